```python
import math
import jax, jax.numpy as jnp
from jax import lax
import numpy as np

D_MODEL = 1024
BATCH = 4
SEQ = 4096
DEPTH = 1
DEC_BATCH = 128
DEC_SEQ = 1
PAST_LEN = 8192
PAGE_SIZE = 128

MOBA_HEADS = 8
MOBA_HEAD_DIM = 64
MOBA_WIDTH = MOBA_HEADS * MOBA_HEAD_DIM
MOBA_BLOCK = 256
MOBA_TOPK = 3
MOBA_ROT_DIMS = MOBA_HEAD_DIM // 4
ROPE_THETA = 500000.0
MLA_HEADS = 8
MLA_NOPE_DIM = 64
MLA_ROPE_DIM = 32
MLA_V_DIM = 64
MLA_OUT_WIDTH = MLA_HEADS * MLA_V_DIM
MLA_Q_RANK = 384
MLA_KV_RANK = 256
MLA_ROPE_THETA = 10000.0
N_MEM = 256
XATTN_HEADS = 4
XATTN_HEAD_DIM = 128
XATTN_WIDTH = XATTN_HEADS * XATTN_HEAD_DIM
D_FF = 2816
QUERY_BLOCK = 128
RMS_EPS = 1e-6
W_IN_COLS = 3 * MOBA_WIDTH + MLA_Q_RANK + MLA_KV_RANK + MLA_ROPE_DIM + 2 * D_MODEL

kernel_name = 'moba_mla_gated_hybrid_step'


def rmsnorm(x, g):
    x32 = x.astype(jnp.float32)
    y = x32 * lax.rsqrt(jnp.mean(x32 * x32, axis=-1, keepdims=True) + RMS_EPS)
    return (y * g.astype(jnp.float32)).astype(x.dtype)


def swiglu(h, w_gate, w_up, w_down):
    return (jax.nn.silu(h @ w_gate) * (h @ w_up)) @ w_down


def rope(x, pos, theta, n_rot):
    half = n_rot // 2
    inv = theta ** (-jnp.arange(half, dtype=jnp.float32) * 2.0 / n_rot)
    ang = pos.astype(jnp.float32)[:, None] * inv[None, :]
    cos = jnp.cos(ang)[:, None, :]
    sin = jnp.sin(ang)[:, None, :]
    xr = x[..., :n_rot].astype(jnp.float32)
    x1, x2 = xr[..., :half], xr[..., half:]
    rot = jnp.concatenate([x1 * cos - x2 * sin, x1 * sin + x2 * cos], axis=-1).astype(x.dtype)
    return jnp.concatenate([rot, x[..., n_rot:]], axis=-1)


def gather_pages(pool, page_table):
    rows = pool[page_table]
    return rows.reshape((page_table.shape[0], page_table.shape[1] * pool.shape[1]) + pool.shape[2:])


def pad_to_blocks(x):
    t = x.shape[1]
    tp = -(-t // MOBA_BLOCK) * MOBA_BLOCK
    return jnp.pad(x, ((0, 0), (0, tp - t), (0, 0), (0, 0)))


def block_means(k_full):
    b, tp, h, d = k_full.shape
    kb = k_full.reshape(b, tp // MOBA_BLOCK, MOBA_BLOCK, h, d).astype(jnp.float32)
    return jnp.mean(kb, axis=2).astype(k_full.dtype)


def moba_attend(q, q_pos, kmean, fetch):
    b, s, h, d = q.shape
    nb = kmean.shape[1]
    k_top = min(MOBA_TOPK, nb)
    cur = q_pos // MOBA_BLOCK
    gate = jnp.einsum('bshd,bnhd->bhsn', q, kmean).astype(jnp.float32)
    past = jnp.arange(nb)[None, :] < cur[:, None]
    gate = jnp.where(past, gate, -jnp.inf)
    top_val, top_idx = lax.top_k(gate, k_top)
    offs = jnp.arange(MOBA_BLOCK, dtype=jnp.int32)
    pos_sel = (top_idx[..., None] * MOBA_BLOCK + offs).reshape(b, h, s, k_top * MOBA_BLOCK)
    ok_sel = jnp.broadcast_to(jnp.isfinite(top_val)[..., None], (b, h, s, k_top, MOBA_BLOCK))
    ok_sel = ok_sel.reshape(b, h, s, k_top * MOBA_BLOCK)
    pos_own = cur[:, None] * MOBA_BLOCK + offs[None, :]
    ok_own = pos_own <= q_pos[:, None]
    pos_all = jnp.concatenate([pos_sel, jnp.broadcast_to(pos_own, (b, h, s, MOBA_BLOCK))], axis=-1)
    ok_all = jnp.concatenate([ok_sel, jnp.broadcast_to(ok_own, (b, h, s, MOBA_BLOCK))], axis=-1)
    b_idx = jnp.arange(b)[:, None, None, None]
    h_idx = jnp.arange(h)[None, :, None, None]
    k_rows, v_rows = fetch(pos_all, b_idx, h_idx)
    sc = jnp.einsum('bshd,bhsld->bhsl', q, k_rows).astype(jnp.float32) * (1.0 / math.sqrt(d))
    p = jax.nn.softmax(jnp.where(ok_all, sc, -jnp.inf), axis=-1).astype(v_rows.dtype)
    return jnp.einsum('bhsl,bhsld->bshd', p, v_rows)


def moba_prompt(q, k, v, pos):
    b, t, h, d = q.shape
    k_full = pad_to_blocks(k)
    v_full = pad_to_blocks(v)
    kmean = block_means(k_full)

    def fetch(p, bi, hi):
        return k_full[bi, p, hi], v_full[bi, p, hi]

    nc = t // QUERY_BLOCK
    qc = q.reshape(b, nc, QUERY_BLOCK, h, d).swapaxes(0, 1)
    pc = pos.reshape(nc, QUERY_BLOCK)
    out = lax.map(lambda a: moba_attend(a[0], a[1], kmean, fetch), (qc, pc))
    return out.swapaxes(0, 1).reshape(b, t, h, d)


def moba_sample(q, k_new, v_new, pool_k, pool_v, page_table):
    b, s, h, d = q.shape
    n_pages = page_table.shape[1]
    past_len = n_pages * PAGE_SIZE
    k_full = pad_to_blocks(jnp.concatenate([gather_pages(pool_k, page_table), k_new], axis=1))
    kmean = block_means(k_full)

    def fetch(p, bi, hi):
        k_rows = k_full[bi, p, hi]
        in_past = p < past_len
        phys = page_table[bi, jnp.minimum(p // PAGE_SIZE, n_pages - 1)]
        v_past = pool_v[phys, p % PAGE_SIZE, hi]
        v_cur = v_new[bi, jnp.clip(p - past_len, 0, s - 1), hi]
        return k_rows, jnp.where(in_past[..., None], v_past, v_cur)

    q_pos = past_len + jnp.arange(s, dtype=jnp.int32)
    return moba_attend(q, q_pos, kmean, fetch)


def mla_attend(q_nope, q_rope, ckv, krope, q_pos, w_uk, w_uv):
    q_lat = jnp.einsum('bshn,chn->bshc', q_nope, w_uk)
    sc = jnp.einsum('bshc,btc->bhst', q_lat, ckv) + jnp.einsum('bshr,btr->bhst', q_rope, krope)
    sc = sc.astype(jnp.float32) * (1.0 / math.sqrt(MLA_NOPE_DIM + MLA_ROPE_DIM))
    causal = jnp.arange(ckv.shape[1])[None, :] <= q_pos[:, None]
    p = jax.nn.softmax(jnp.where(causal, sc, -jnp.inf), axis=-1).astype(ckv.dtype)
    o_lat = jnp.einsum('bhst,btc->bshc', p, ckv)
    return jnp.einsum('bshc,chv->bshv', o_lat, w_uv)


def mla_prompt(q_nope, q_rope, ckv, krope, pos, w_uk, w_uv):
    b, t, h, _ = q_nope.shape
    nc = t // QUERY_BLOCK
    qn = q_nope.reshape(b, nc, QUERY_BLOCK, h, MLA_NOPE_DIM).swapaxes(0, 1)
    qr = q_rope.reshape(b, nc, QUERY_BLOCK, h, MLA_ROPE_DIM).swapaxes(0, 1)
    pc = pos.reshape(nc, QUERY_BLOCK)
    out = lax.map(lambda a: mla_attend(a[0], a[1], ckv, krope, a[2], w_uk, w_uv), (qn, qr, pc))
    return out.swapaxes(0, 1).reshape(b, t, h, MLA_V_DIM)


def mla_sample(q_nope, q_rope, ckv_new, kr_new, pool_ckv, pool_kr, page_table, w_uk, w_uv):
    s = q_nope.shape[1]
    past_len = page_table.shape[1] * PAGE_SIZE
    ckv = jnp.concatenate([gather_pages(pool_ckv, page_table), ckv_new], axis=1)
    kr = jnp.concatenate([gather_pages(pool_kr, page_table), kr_new], axis=1)
    q_pos = past_len + jnp.arange(s, dtype=jnp.int32)
    return mla_attend(q_nope, q_rope, ckv, kr, q_pos, w_uk, w_uv)


def mem_kv(mem, g, w_k, w_v):
    b, m, _ = mem.shape
    hm = rmsnorm(mem, g)
    return ((hm @ w_k).reshape(b, m, XATTN_HEADS, XATTN_HEAD_DIM),
            (hm @ w_v).reshape(b, m, XATTN_HEADS, XATTN_HEAD_DIM))


def mem_attend(h, mk, mv, w_q, w_o):
    b, s, _ = h.shape
    q = (h @ w_q).reshape(b, s, XATTN_HEADS, XATTN_HEAD_DIM)
    sc = jnp.einsum('bshd,bmhd->bhsm', q, mk).astype(jnp.float32) * (1.0 / math.sqrt(XATTN_HEAD_DIM))
    p = jax.nn.softmax(sc, axis=-1).astype(mv.dtype)
    o = jnp.einsum('bhsm,bmhd->bshd', p, mv).reshape(b, s, XATTN_WIDTH)
    return o @ w_o


def mixer_inputs(h, pos, w_in, q_norm, w_uq, kv_norm):
    b, s, _ = h.shape
    sizes = (MOBA_WIDTH, MOBA_WIDTH, MOBA_WIDTH, MLA_Q_RANK, MLA_KV_RANK, MLA_ROPE_DIM, D_MODEL)
    bounds = []
    acc = 0
    for n in sizes:
        acc += n
        bounds.append(acc)
    q, k, v, cq, ckv, kr, ga, gb = jnp.split(h @ w_in, bounds, axis=-1)
    q = rope(q.reshape(b, s, MOBA_HEADS, MOBA_HEAD_DIM), pos, ROPE_THETA, MOBA_ROT_DIMS)
    k = rope(k.reshape(b, s, MOBA_HEADS, MOBA_HEAD_DIM), pos, ROPE_THETA, MOBA_ROT_DIMS)
    v = v.reshape(b, s, MOBA_HEADS, MOBA_HEAD_DIM)
    qh = (rmsnorm(cq, q_norm) @ w_uq).reshape(b, s, MLA_HEADS, MLA_NOPE_DIM + MLA_ROPE_DIM)
    q_nope = qh[..., :MLA_NOPE_DIM]
    q_rope = rope(qh[..., MLA_NOPE_DIM:], pos, MLA_ROPE_THETA, MLA_ROPE_DIM)
    ckv = rmsnorm(ckv, kv_norm)
    kr = rope(kr[:, :, None, :], pos, MLA_ROPE_THETA, MLA_ROPE_DIM)[:, :, 0, :]
    return q, k, v, q_nope, q_rope, ckv, kr, ga, gb


def run_layer(x, pos, mk, mv, lw, moba_fn, mla_fn):
    b, s, _ = x.shape
    x = x + 0.5 * swiglu(rmsnorm(x, lw['ffn1_norm']), lw['ffn1_w_gate'], lw['ffn1_w_up'], lw['ffn1_w_down'])
    h = rmsnorm(x, lw['mix_norm'])
    q, k, v, qn, qr, ckv, kr, ga, gb = mixer_inputs(h, pos, lw['w_in'], lw['mla_q_norm'], lw['mla_w_uq'], lw['mla_kv_norm'])
    o_a = moba_fn(q, k, v).reshape(b, s, MOBA_WIDTH) @ lw['moba_w_o']
    o_b = mla_fn(qn, qr, ckv, kr).reshape(b, s, MLA_OUT_WIDTH) @ lw['mla_w_o']
    x = x + (jax.nn.sigmoid(ga) * o_a + jax.nn.sigmoid(gb) * o_b) @ lw['w_out']
    x = x + mem_attend(rmsnorm(x, lw['xattn_norm']), mk, mv, lw['xattn_w_q'], lw['xattn_w_o'])
    x = x + 0.5 * swiglu(rmsnorm(x, lw['ffn2_norm']), lw['ffn2_w_gate'], lw['ffn2_w_up'], lw['ffn2_w_down'])
    return x, k, v, ckv, kr


def setup_inputs(seed: int = 0) -> dict:
    key = jax.random.key(seed)
    ks = iter(jax.random.split(key, 48))
    f32 = jnp.float32

    def nrm(shape, scale):
        return jax.random.normal(next(ks), shape, f32) * scale

    def gain(shape):
        return 1.0 + 0.01 * jax.random.normal(next(ks), shape, f32)

    n_pages = PAST_LEN // PAGE_SIZE
    n_used = DEC_BATCH * n_pages
    n_pool = n_used + n_used // 4
    page_table = jax.random.permutation(next(ks), n_pool)[:n_used].reshape(DEC_BATCH, n_pages).astype(jnp.int32)
    L = DEPTH
    return {
        'x_prompt': nrm((BATCH, SEQ, D_MODEL), 1.0),
        'x_sample': nrm((DEC_BATCH, DEC_SEQ, D_MODEL), 1.0),
        'cache_moba_k': nrm((L, n_pool, PAGE_SIZE, MOBA_HEADS, MOBA_HEAD_DIM), 1.0),
        'cache_moba_v': nrm((L, n_pool, PAGE_SIZE, MOBA_HEADS, MOBA_HEAD_DIM), 1.0),
        'cache_mla_ckv': nrm((L, n_pool, PAGE_SIZE, MLA_KV_RANK), 1.0),
        'cache_mla_krope': nrm((L, n_pool, PAGE_SIZE, MLA_ROPE_DIM), 1.0),
        'cache_mem_k': nrm((L, DEC_BATCH, N_MEM, XATTN_HEADS, XATTN_HEAD_DIM), 1.0),
        'cache_mem_v': nrm((L, DEC_BATCH, N_MEM, XATTN_HEADS, XATTN_HEAD_DIM), 1.0),
        'page_table': page_table,
        'mem_prompt': nrm((BATCH, N_MEM, D_MODEL), 1.0),
        'ffn1_norm': gain((L, D_MODEL)),
        'ffn1_w_gate': nrm((L, D_MODEL, D_FF), D_MODEL ** -0.5),
        'ffn1_w_up': nrm((L, D_MODEL, D_FF), D_MODEL ** -0.5),
        'ffn1_w_down': nrm((L, D_FF, D_MODEL), D_FF ** -0.5),
        'mix_norm': gain((L, D_MODEL)),
        'w_in': nrm((L, D_MODEL, W_IN_COLS), D_MODEL ** -0.5),
        'mla_q_norm': gain((L, MLA_Q_RANK)),
        'mla_w_uq': nrm((L, MLA_Q_RANK, MLA_HEADS * (MLA_NOPE_DIM + MLA_ROPE_DIM)), MLA_Q_RANK ** -0.5),
        'mla_kv_norm': gain((L, MLA_KV_RANK)),
        'mla_w_uk': nrm((L, MLA_KV_RANK, MLA_HEADS, MLA_NOPE_DIM), MLA_KV_RANK ** -0.5),
        'mla_w_uv': nrm((L, MLA_KV_RANK, MLA_HEADS, MLA_V_DIM), MLA_KV_RANK ** -0.5),
        'moba_w_o': nrm((L, MOBA_WIDTH, D_MODEL), MOBA_WIDTH ** -0.5),
        'mla_w_o': nrm((L, MLA_OUT_WIDTH, D_MODEL), MLA_OUT_WIDTH ** -0.5),
        'w_out': nrm((L, D_MODEL, D_MODEL), D_MODEL ** -0.5),
        'xattn_norm': gain((L, D_MODEL)),
        'mem_norm': gain((L, D_MODEL)),
        'xattn_w_q': nrm((L, D_MODEL, XATTN_WIDTH), D_MODEL ** -0.5),
        'xattn_w_k': nrm((L, D_MODEL, XATTN_WIDTH), D_MODEL ** -0.5),
        'xattn_w_v': nrm((L, D_MODEL, XATTN_WIDTH), D_MODEL ** -0.5),
        'xattn_w_o': nrm((L, XATTN_WIDTH, D_MODEL), XATTN_WIDTH ** -0.5),
        'ffn2_norm': gain((L, D_MODEL)),
        'ffn2_w_gate': nrm((L, D_MODEL, D_FF), D_MODEL ** -0.5),
        'ffn2_w_up': nrm((L, D_MODEL, D_FF), D_MODEL ** -0.5),
        'ffn2_w_down': nrm((L, D_FF, D_MODEL), D_FF ** -0.5),
        'final_norm': gain((D_MODEL,)),
    }


def reference(x_prompt, x_sample, cache_moba_k, cache_moba_v, cache_mla_ckv, cache_mla_krope,
              cache_mem_k, cache_mem_v, page_table, mem_prompt,
              ffn1_norm, ffn1_w_gate, ffn1_w_up, ffn1_w_down,
              mix_norm, w_in, mla_q_norm, mla_w_uq, mla_kv_norm, mla_w_uk, mla_w_uv,
              moba_w_o, mla_w_o, w_out,
              xattn_norm, mem_norm, xattn_w_q, xattn_w_k, xattn_w_v, xattn_w_o,
              ffn2_norm, ffn2_w_gate, ffn2_w_up, ffn2_w_down, final_norm):
    pos_p = jnp.arange(x_prompt.shape[1], dtype=jnp.int32)
    past_len = page_table.shape[1] * PAGE_SIZE
    pos_s = past_len + jnp.arange(x_sample.shape[1], dtype=jnp.int32)
    xp, xs = x_prompt, x_sample
    p_k, p_v, p_c, p_r, p_mk, p_mv = [], [], [], [], [], []
    s_k, s_v, s_c, s_r = [], [], [], []
    for l in range(DEPTH):
        lw = {
            'ffn1_norm': ffn1_norm[l], 'ffn1_w_gate': ffn1_w_gate[l], 'ffn1_w_up': ffn1_w_up[l],
            'ffn1_w_down': ffn1_w_down[l], 'mix_norm': mix_norm[l], 'w_in': w_in[l],
            'mla_q_norm': mla_q_norm[l], 'mla_w_uq': mla_w_uq[l], 'mla_kv_norm': mla_kv_norm[l],
            'moba_w_o': moba_w_o[l], 'mla_w_o': mla_w_o[l], 'w_out': w_out[l],
            'xattn_norm': xattn_norm[l], 'xattn_w_q': xattn_w_q[l], 'xattn_w_o': xattn_w_o[l],
            'ffn2_norm': ffn2_norm[l], 'ffn2_w_gate': ffn2_w_gate[l], 'ffn2_w_up': ffn2_w_up[l],
            'ffn2_w_down': ffn2_w_down[l],
        }
        w_uk, w_uv = mla_w_uk[l], mla_w_uv[l]
        mk_p, mv_p = mem_kv(mem_prompt, mem_norm[l], xattn_w_k[l], xattn_w_v[l])
        xp, kp, vp, cp, rp = run_layer(
            xp, pos_p, mk_p, mv_p, lw,
            lambda qq, kk, vv: moba_prompt(qq, kk, vv, pos_p),
            lambda qn, qr, cc, rr: mla_prompt(qn, qr, cc, rr, pos_p, w_uk, w_uv))
        p_k.append(kp); p_v.append(vp); p_c.append(cp); p_r.append(rp)
        p_mk.append(mk_p); p_mv.append(mv_p)
        pk_l, pv_l, pc_l, pr_l = cache_moba_k[l], cache_moba_v[l], cache_mla_ckv[l], cache_mla_krope[l]
        xs, ks_, vs_, cs_, rs_ = run_layer(
            xs, pos_s, cache_mem_k[l], cache_mem_v[l], lw,
            lambda qq, kk, vv: moba_sample(qq, kk, vv, pk_l, pv_l, page_table),
            lambda qn, qr, cc, rr: mla_sample(qn, qr, cc, rr, pc_l, pr_l, page_table, w_uk, w_uv))
        s_k.append(ks_); s_v.append(vs_); s_c.append(cs_); s_r.append(rs_)
    y_prompt = rmsnorm(xp, final_norm)
    y_sample = rmsnorm(xs, final_norm)
    return (y_prompt, y_sample,
            jnp.stack(p_k), jnp.stack(p_v), jnp.stack(p_c), jnp.stack(p_r), jnp.stack(p_mk), jnp.stack(p_mv),
            jnp.stack(s_k), jnp.stack(s_v), jnp.stack(s_c), jnp.stack(s_r))
```

```python
import functools
import math

import jax
import jax.numpy as jnp
from jax import lax
from jax.experimental import pallas as pl
from jax.experimental.pallas import tpu as pltpu

F32 = jnp.float32
BF16 = jnp.bfloat16

D_MODEL = 1024
PAGE = 128
HEADS = 8
HD = 64
WIDTH = HEADS * HD
MOBA_BLOCK = 256
MOBA_TOPK = 3
MOBA_ROT = HD // 4
MOBA_THETA = 500000.0
MLA_ROPE = 32
MLA_THETA = 10000.0
MLA_Q_RANK = 384
MLA_KV_RANK = 256
XHEADS = 4
XHD = 128
D_FF = 2816
RMS_EPS = 1e-6
LANES = 128
NEG = -1e30
VMEM_LIMIT = 56 * 1024 * 1024

MOBA_SCALE = 1.0 / math.sqrt(HD)
MLA_SCALE = 1.0 / math.sqrt(HD + MLA_ROPE)
X_SCALE = 1.0 / math.sqrt(XHD)

C_Q, C_K, C_V, C_CQ, C_CKV, C_KR, C_END = 0, 512, 1024, 1536, 1920, 2176, 2304


def _cparams(n_axes=1, vmem=VMEM_LIMIT):
    return pltpu.CompilerParams(dimension_semantics=("arbitrary",) * n_axes, vmem_limit_bytes=vmem)


def _const_spec(shape):
    nd = len(shape)
    return pl.BlockSpec(shape, lambda *_: (0,) * nd, pipeline_mode=pl.Buffered(1))


def _dot(a, b):
    return jnp.dot(a, b, preferred_element_type=F32)


def _dot_nt(a, b):
    return lax.dot_general(a, b, (((1,), (1,)), ((), ())), preferred_element_type=F32)


def _split3(a):
    hi = a.astype(BF16)
    r1 = a - hi.astype(F32)
    mid = r1.astype(BF16)
    lo = (r1 - mid.astype(F32)).astype(BF16)
    return hi, mid, lo


def _dot_nt_f32(a, b):
    ah, am, al = _split3(a)
    bh, bm, bl = _split3(b)
    return (_dot_nt(ah, bh) + (_dot_nt(ah, bm) + _dot_nt(am, bh))
            + (_dot_nt(ah, bl) + _dot_nt(am, bm) + _dot_nt(al, bh)))


def _rms(x, g):
    ms = jnp.mean(x * x, axis=-1, keepdims=True)
    return x * lax.rsqrt(ms + RMS_EPS) * g


def _sigmoid(x):
    return 1.0 / (1.0 + jnp.exp(-x))


def _rope_block(x, c, a, b, half):
    up = pltpu.roll(x, LANES - half, 1)
    dn = pltpu.roll(x, half, 1)
    return x * c + up * a + dn * b


def _rope_wide(x, tabs, half):
    c, a, b = tabs
    blocks = [_rope_block(x[:, j:j + LANES], c, a, b, half) for j in range(0, x.shape[1], LANES)]
    return blocks[0] if len(blocks) == 1 else jnp.concatenate(blocks, axis=1)


def _ffn_kernel(*refs, pre_proj, final_norm, n_chunks):
    it = iter(refs)
    x_ref = next(it)
    if pre_proj:
        o_ref, wo_ref = next(it), next(it)
    g_ref, wg_ref, wu_ref, wd_ref = next(it), next(it), next(it), next(it)
    fn_ref = next(it) if final_norm else None
    out_ref = next(it)

    x = x_ref[...]
    if pre_proj:
        x = x + _dot(o_ref[...], wo_ref[...])
    h = _rms(x, g_ref[...]).astype(BF16)
    fc = D_FF // n_chunks
    acc = jnp.zeros(x.shape, F32)
    for c in range(n_chunks):
        gt = _dot(h, wg_ref[:, c * fc:(c + 1) * fc])
        up = _dot(h, wu_ref[:, c * fc:(c + 1) * fc])
        act = (gt * _sigmoid(gt) * up).astype(BF16)
        acc = acc + _dot(act, wd_ref[c * fc:(c + 1) * fc, :])
    y = x + 0.5 * acc
    if final_norm:
        y = _rms(y, fn_ref[...])
    out_ref[...] = y


def _ffn(x, g, wg, wu, wd, *, tm, pre=None, fnorm=None, name):
    t = x.shape[0]
    args, specs = [x], [pl.BlockSpec((tm, D_MODEL), lambda i: (i, 0))]
    if pre is not None:
        o, wo = pre
        args += [o, wo]
        specs += [pl.BlockSpec((tm, o.shape[1]), lambda i: (i, 0)), _const_spec(wo.shape)]
    args += [g, wg, wu, wd]
    specs += [_const_spec(g.shape), _const_spec(wg.shape), _const_spec(wu.shape), _const_spec(wd.shape)]
    if fnorm is not None:
        args.append(fnorm)
        specs.append(_const_spec(fnorm.shape))
    kern = functools.partial(_ffn_kernel, pre_proj=pre is not None, final_norm=fnorm is not None, n_chunks=2)
    return pl.pallas_call(
        kern, grid=(t // tm,), in_specs=specs,
        out_specs=pl.BlockSpec((tm, D_MODEL), lambda i: (i, 0)),
        out_shape=jax.ShapeDtypeStruct((t, D_MODEL), F32),
        compiler_params=_cparams(1), name=name)(*args)


def _mix_prompt_kernel(x_ref, g_ref, w_ref, qn_ref, wuq_ref, kvn_ref, wuk_ref, wuv_ref,
                       mc_ref, ma_ref, mb_ref, lc_ref, la_ref, lb_ref,
                       qf_ref, kb_ref, kt_ref, vt_ref, vt4_ref, km_ref,
                       qcat_ref, kcat_ref, vtm4_ref, ckv_ref, krt_ref, *, tm):
    h = _rms(x_ref[...], g_ref[...]).astype(BF16)
    proj = _dot(h, w_ref[...])
    mtab = (mc_ref[...], ma_ref[...], mb_ref[...])
    ltab = (lc_ref[...], la_ref[...], lb_ref[...])
    nblk = tm // MOBA_BLOCK

    q = _rope_wide(proj[:, C_Q:C_K], mtab, MOBA_ROT // 2)
    k = _rope_wide(proj[:, C_K:C_V], mtab, MOBA_ROT // 2)
    v = proj[:, C_V:C_CQ]
    qf_ref[...] = q
    kb_ref[...] = k.astype(BF16)
    kt_ref[0] = k.T
    vt = v.T
    vt_ref[0] = vt
    for j in range(nblk):
        vt4_ref[0, j] = vt[:, j * MOBA_BLOCK:(j + 1) * MOBA_BLOCK].astype(BF16)
        km_ref[0, j:j + 1, :] = jnp.sum(k[j * MOBA_BLOCK:(j + 1) * MOBA_BLOCK], axis=0, keepdims=True) * (1.0 / MOBA_BLOCK)

    cqn = _rms(proj[:, C_CQ:C_CKV], qn_ref[...]).astype(BF16)
    qh = _rope_wide(_dot(cqn, wuq_ref[...]), ltab, MLA_ROPE // 2)
    qcat_ref[...] = (qh * MLA_SCALE).astype(BF16)

    ckvn = _rms(proj[:, C_CKV:C_KR], kvn_ref[...])
    ckv_ref[...] = ckvn
    kr = _rope_block(proj[:, C_KR:C_END], *ltab, MLA_ROPE // 2)
    krt_ref[0] = kr.T[HD:HD + MLA_ROPE, :]
    cb = ckvn.astype(BF16)
    kn = _dot(cb, wuk_ref[...])
    kcat_ref[...] = (kn + jnp.concatenate([kr] * HEADS, axis=1)).astype(BF16)
    vtm = _dot(cb, wuv_ref[...]).T
    for j in range(nblk):
        vtm4_ref[0, j] = vtm[:, j * MOBA_BLOCK:(j + 1) * MOBA_BLOCK].astype(BF16)


def _mix_prompt(x, g, w, qn, wuq, kvn, wuk, wuv, tabs, *, bsz, seq, tm):
    t = bsz * seq
    spt = seq // tm
    nblk = tm // MOBA_BLOCK
    nb = seq // MOBA_BLOCK
    row = lambda i: (i, 0)
    tab_spec = pl.BlockSpec((tm, LANES), lambda i: (i % spt, 0))
    in_specs = [pl.BlockSpec((tm, D_MODEL), row), _const_spec(g.shape), _const_spec(w.shape),
                _const_spec(qn.shape), _const_spec(wuq.shape), _const_spec(kvn.shape),
                _const_spec(wuk.shape), _const_spec(wuv.shape)] + [tab_spec] * 6
    tcol = lambda i: (i // spt, 0, i % spt)
    t4 = lambda i: (i // spt, i % spt, 0, 0)
    out_shape = [
        jax.ShapeDtypeStruct((t, WIDTH), F32),
        jax.ShapeDtypeStruct((t, WIDTH), BF16),
        jax.ShapeDtypeStruct((bsz, WIDTH, seq), F32),
        jax.ShapeDtypeStruct((bsz, WIDTH, seq), F32),
        jax.ShapeDtypeStruct((bsz, nb, WIDTH, MOBA_BLOCK), BF16),
        jax.ShapeDtypeStruct((t // tm, nblk, WIDTH), F32),
        jax.ShapeDtypeStruct((t, HEADS * LANES), BF16),
        jax.ShapeDtypeStruct((t, HEADS * LANES), BF16),
        jax.ShapeDtypeStruct((bsz, nb, WIDTH, MOBA_BLOCK), BF16),
        jax.ShapeDtypeStruct((t, MLA_KV_RANK), F32),
        jax.ShapeDtypeStruct((bsz, MLA_ROPE, seq), F32),
    ]
    out_specs = [
        pl.BlockSpec((tm, WIDTH), row), pl.BlockSpec((tm, WIDTH), row),
        pl.BlockSpec((1, WIDTH, tm), tcol), pl.BlockSpec((1, WIDTH, tm), tcol),
        pl.BlockSpec((1, nblk, WIDTH, MOBA_BLOCK), t4),
        pl.BlockSpec((1, nblk, WIDTH), lambda i: (i, 0, 0)),
        pl.BlockSpec((tm, HEADS * LANES), row), pl.BlockSpec((tm, HEADS * LANES), row),
        pl.BlockSpec((1, nblk, WIDTH, MOBA_BLOCK), t4),
        pl.BlockSpec((tm, MLA_KV_RANK), row),
        pl.BlockSpec((1, MLA_ROPE, tm), tcol),
    ]
    return pl.pallas_call(
        functools.partial(_mix_prompt_kernel, tm=tm), grid=(t // tm,), in_specs=in_specs,
        out_specs=out_specs, out_shape=out_shape, compiler_params=_cparams(1), name="mix_prompt",
    )(x, g, w, qn, wuq, kvn, wuk, wuv, *tabs)


def _mix_sample_kernel(x_ref, g_ref, w_ref, qn_ref, wuq_ref, kvn_ref, wqlat_ref, pr_ref, pk_ref,
                       mc_ref, ma_ref, mb_ref, lc_ref, la_ref, lb_ref,
                       qt_ref, kt_ref, vt_ref, ckv_ref, krt_ref, ql_ref, qr_ref, krow_ref):
    h = _rms(x_ref[...], g_ref[...]).astype(BF16)
    proj = _dot(h, w_ref[...])
    mtab = (mc_ref[...], ma_ref[...], mb_ref[...])
    ltab = (lc_ref[...], la_ref[...], lb_ref[...])
    qt_ref[...] = _rope_wide(proj[:, C_Q:C_K], mtab, MOBA_ROT // 2).T
    kt_ref[...] = _rope_wide(proj[:, C_K:C_V], mtab, MOBA_ROT // 2).T
    vt_ref[...] = proj[:, C_V:C_CQ].T

    cqn = _rms(proj[:, C_CQ:C_CKV], qn_ref[...]).astype(BF16)
    qh = (_rope_wide(_dot(cqn, wuq_ref[...]), ltab, MLA_ROPE // 2) * MLA_SCALE).astype(BF16)
    ql_ref[...] = _dot(qh, wqlat_ref[...])
    qr_ref[...] = _dot(qh, pr_ref[...])

    ckv_ref[...] = _rms(proj[:, C_CKV:C_KR], kvn_ref[...])
    kr = _rope_block(proj[:, C_KR:C_END], *ltab, MLA_ROPE // 2)
    krt_ref[...] = kr.T[HD:HD + MLA_ROPE, :]
    krow_ref[...] = _dot(kr.astype(BF16), pk_ref[...])


def _mix_sample(x, g, w, qn, wuq, kvn, wqlat, pr, pk, tabs):
    n = x.shape[0]
    args = (x, g, w, qn, wuq, kvn, wqlat, pr, pk) + tuple(tabs)
    out_shape = [
        jax.ShapeDtypeStruct((WIDTH, n), F32), jax.ShapeDtypeStruct((WIDTH, n), F32),
        jax.ShapeDtypeStruct((WIDTH, n), F32), jax.ShapeDtypeStruct((n, MLA_KV_RANK), F32),
        jax.ShapeDtypeStruct((MLA_ROPE, n), F32), jax.ShapeDtypeStruct((n, HEADS * MLA_KV_RANK), F32),
        jax.ShapeDtypeStruct((n, HEADS * MLA_ROPE), F32), jax.ShapeDtypeStruct((n, LANES), F32),
    ]
    return pl.pallas_call(
        _mix_sample_kernel, grid=(1,), in_specs=[_const_spec(a.shape) for a in args],
        out_specs=[_const_spec(s.shape) for s in out_shape], out_shape=out_shape,
        compiler_params=_cparams(1), name="mix_sample")(*args)


def _attn_prompt_kernel(*refs, moba):
    if moba:
        q_ref, k_ref, vt_ref, km_ref, o_ref, bias_ref = refs
    else:
        q_ref, k_ref, vt_ref, o_ref = refs
    hd_i = pl.program_id(1)
    qi = pl.program_id(2)
    blk = MOBA_BLOCK

    if moba:
        nb = km_ref.shape[1]
        lane = lax.broadcasted_iota(jnp.int32, (1, LANES), 1)
        lo = (hd_i % 2) * HD
        qh = jnp.where((lane >= lo) & (lane < lo + HD), q_ref[...], 0.0)
        gate = _dot_nt_f32(km_ref[0], qh)
        bidx = lax.broadcasted_iota(jnp.int32, (nb, 1), 0)
        past = bidx < qi
        gate = jnp.where(past, gate, -jnp.inf)
        cnt = jnp.zeros(gate.shape, jnp.int32)
        for m in range(nb):
            gm = gate[m:m + 1, :]
            beats = (gm > gate) | ((gm == gate) & (m < bidx))
            cnt = cnt + jnp.where(beats, 1, 0)
        bias_ref[...] = jnp.where(past & (cnt < MOBA_TOPK), 0.0, NEG)
        qb = (qh * MOBA_SCALE).astype(BF16)
    else:
        qb = q_ref[...]

    def block_scores(j):
        start = pl.multiple_of(j * blk, blk)
        return _dot_nt(k_ref[pl.ds(start, blk), :], qb)

    s = block_scores(qi)
    kpos = lax.broadcasted_iota(jnp.int32, (blk, blk), 0)
    qpos = lax.broadcasted_iota(jnp.int32, (blk, blk), 1)
    s = jnp.where(kpos <= qpos, s, NEG)
    m0 = jnp.max(s, axis=0, keepdims=True)
    p = jnp.exp(s - m0)
    l0 = jnp.sum(p, axis=0, keepdims=True)
    acc0 = _dot(vt_ref[0, qi].astype(BF16), p.astype(BF16))

    def body(j, carry):
        m, l, acc = carry
        sj = block_scores(j)
        if moba:
            sj = sj + bias_ref[pl.ds(j, 1), :]
        mn = jnp.maximum(m, jnp.max(sj, axis=0, keepdims=True))
        alpha = jnp.exp(m - mn)
        pj = jnp.exp(sj - mn)
        l = l * alpha + jnp.sum(pj, axis=0, keepdims=True)
        acc = acc * alpha + _dot(vt_ref[0, j].astype(BF16), pj.astype(BF16))
        return mn, l, acc

    m, l, acc = lax.fori_loop(0, qi, body, (m0, l0, acc0))
    o_ref[0] = acc / l


def _attn_prompt(q, k, vt4, kmean, *, bsz, seq, moba):
    nq = seq // MOBA_BLOCK
    if moba:
        q_spec = pl.BlockSpec((MOBA_BLOCK, LANES), lambda b, h, i: (b * nq + i, h // 2))
        k_spec = pl.BlockSpec((seq, LANES), lambda b, h, i: (b, h // 2))
    else:
        q_spec = pl.BlockSpec((MOBA_BLOCK, LANES), lambda b, h, i: (b * nq + i, h))
        k_spec = pl.BlockSpec((seq, LANES), lambda b, h, i: (b, h))
    v_spec = pl.BlockSpec((1, nq, HD, MOBA_BLOCK), lambda b, h, i: (b, 0, h, 0))
    args, specs, scratch = [q, k, vt4], [q_spec, k_spec, v_spec], []
    if moba:
        args.append(kmean)
        specs.append(pl.BlockSpec((1, nq, LANES), lambda b, h, i: (b, 0, h // 2)))
        scratch.append(pltpu.VMEM((nq, MOBA_BLOCK), F32))
    return pl.pallas_call(
        functools.partial(_attn_prompt_kernel, moba=moba), grid=(bsz, HEADS, nq),
        in_specs=specs, out_specs=pl.BlockSpec((1, HD, MOBA_BLOCK), lambda b, h, i: (b, h, i)),
        out_shape=jax.ShapeDtypeStruct((bsz, WIDTH, seq), F32), scratch_shapes=scratch,
        compiler_params=_cparams(3), name="attn_moba" if moba else "attn_mla")(*args)


def _merge_kernel(*refs, b_latent):
    it = iter(refs)
    x_ref, oa_ref, ob_ref = next(it), next(it), next(it)
    wuvbd_ref = next(it) if b_latent else None
    g_ref, wg_ref, woa_ref, wob_ref, wout_ref, gx_ref, wq_ref, x2_ref, qx_ref = it

    x = x_ref[0]
    h = _rms(x, g_ref[...]).astype(BF16)
    gates = _dot(h, wg_ref[...])
    oa = _dot(oa_ref[0].T.astype(BF16), woa_ref[...])
    if b_latent:
        ob_in = _dot(ob_ref[0].astype(BF16), wuvbd_ref[...]).astype(BF16)
    else:
        ob_in = ob_ref[0].T.astype(BF16)
    ob = _dot(ob_in, wob_ref[...])
    mix = _sigmoid(gates[:, :D_MODEL]) * oa + _sigmoid(gates[:, D_MODEL:]) * ob
    x2 = x + _dot(mix.astype(BF16), wout_ref[...])
    x2_ref[0] = x2
    hx = _rms(x2, gx_ref[...]).astype(BF16)
    qx_ref[0] = (_dot(hx, wq_ref[...]) * X_SCALE).astype(BF16)


def _merge(x3, oa_t, ob, wuvbd, g, wg, woa, wob, wout, gx, wq, *, tm, name):
    bsz, seq, _ = x3.shape
    b_latent = wuvbd is not None
    tok = lambda b, i: (b, i, 0)
    col = lambda b, i: (b, 0, i)
    args = [x3, oa_t, ob]
    specs = [pl.BlockSpec((1, tm, D_MODEL), tok), pl.BlockSpec((1, WIDTH, tm), col)]
    if b_latent:
        specs.append(pl.BlockSpec((1, tm, ob.shape[2]), tok))
        args.append(wuvbd)
        specs.append(_const_spec(wuvbd.shape))
    else:
        specs.append(pl.BlockSpec((1, WIDTH, tm), col))
    consts = [g, wg, woa, wob, wout, gx, wq]
    args += consts
    specs += [_const_spec(a.shape) for a in consts]
    return pl.pallas_call(
        functools.partial(_merge_kernel, b_latent=b_latent), grid=(bsz, seq // tm), in_specs=specs,
        out_specs=[pl.BlockSpec((1, tm, D_MODEL), tok), pl.BlockSpec((1, tm, WIDTH), tok)],
        out_shape=[jax.ShapeDtypeStruct((bsz, seq, D_MODEL), F32), jax.ShapeDtypeStruct((bsz, seq, WIDTH), BF16)],
        compiler_params=_cparams(2), name=name)(*args)


def _memkv_kernel(m_ref, g_ref, wk_ref, wv_ref, k_ref, v_ref, kb_ref, vb_ref):
    hm = _rms(m_ref[0], g_ref[...]).astype(BF16)
    mk = _dot(hm, wk_ref[...])
    mv = _dot(hm, wv_ref[...])
    k_ref[0] = mk
    v_ref[0] = mv
    kb_ref[0] = mk.astype(BF16)
    vb_ref[0] = mv.astype(BF16)


def _memkv(mem, g, wk, wv):
    bsz, nm, _ = mem.shape
    blk = lambda b: (b, 0, 0)
    w = XHEADS * XHD
    return pl.pallas_call(
        _memkv_kernel, grid=(bsz,),
        in_specs=[pl.BlockSpec((1, nm, D_MODEL), blk), _const_spec(g.shape), _const_spec(wk.shape), _const_spec(wv.shape)],
        out_specs=[pl.BlockSpec((1, nm, w), blk)] * 4,
        out_shape=[jax.ShapeDtypeStruct((bsz, nm, w), F32)] * 2 + [jax.ShapeDtypeStruct((bsz, nm, w), BF16)] * 2,
        compiler_params=_cparams(1), name="mem_kv")(mem, g, wk, wv)


def _xattn_prompt_kernel(q_ref, k_ref, v_ref, o_ref):
    outs = []
    for h in range(XHEADS):
        sl = slice(h * XHD, (h + 1) * XHD)
        s = _dot_nt(q_ref[0, :, sl], k_ref[0, :, sl])
        m = jnp.max(s, axis=-1, keepdims=True)
        p = jnp.exp(s - m)
        l = jnp.sum(p, axis=-1, keepdims=True)
        outs.append(_dot(p.astype(BF16), v_ref[0, :, sl]) / l)
    o_ref[0] = jnp.concatenate(outs, axis=1).astype(BF16)


def _xattn_prompt(qx, mk, mv, *, tm):
    bsz, seq, w = qx.shape
    nm = mk.shape[1]
    tok = lambda b, i: (b, i, 0)
    mem = lambda b, i: (b, 0, 0)
    return pl.pallas_call(
        _xattn_prompt_kernel, grid=(bsz, seq // tm),
        in_specs=[pl.BlockSpec((1, tm, w), tok), pl.BlockSpec((1, nm, w), mem), pl.BlockSpec((1, nm, w), mem)],
        out_specs=pl.BlockSpec((1, tm, w), tok), out_shape=jax.ShapeDtypeStruct((bsz, seq, w), BF16),
        compiler_params=_cparams(2), name="xattn_prompt")(qx, mk, mv)


def _xattn_sample_kernel(q_ref, k_ref, v_ref, o_ref):
    q = q_ref[0]
    s = _dot_nt(q, k_ref[0].astype(BF16))
    rows = s.shape[1]
    col = lax.broadcasted_iota(jnp.int32, (XHEADS, rows), 1)
    hd = lax.broadcasted_iota(jnp.int32, (XHEADS, rows), 0)
    s = jnp.where((col % XHEADS) == hd, s, NEG)
    m = jnp.max(s, axis=-1, keepdims=True)
    p = jnp.exp(s - m)
    l = jnp.sum(p, axis=-1, keepdims=True)
    o_ref[0] = _dot(p.astype(BF16), v_ref[0].astype(BF16)) / l


def _xattn_sample(qx, memk, memv):
    n = qx.shape[0]
    rows = memk.shape[1]
    blk = lambda b: (b, 0, 0)
    return pl.pallas_call(
        _xattn_sample_kernel, grid=(n,),
        in_specs=[pl.BlockSpec((1, XHEADS, XHD), blk), pl.BlockSpec((1, rows, XHD), blk), pl.BlockSpec((1, rows, XHD), blk)],
        out_specs=pl.BlockSpec((1, XHEADS, XHD), blk), out_shape=jax.ShapeDtypeStruct((n, XHEADS, XHD), F32),
        compiler_params=_cparams(1), name="xattn_sample")(qx, memk, memv)


def _page_copies(pt_ref, seq, pool_ref, buf_ref, sem, n_pages):
    return [pltpu.make_async_copy(pool_ref.at[pt_ref[seq, pg]], buf_ref.at[pg], sem) for pg in range(n_pages)]


def _moba_sample_score_kernel(pt_ref, qt_ref, ktn_ref, pool_ref, p_ref, idx_ref, ps_ref,
                              kbuf, sem, qcol_ref, s_ref, *, n_pages, n_seq):
    b = pl.program_id(0)
    slot = b % 2
    nblk = n_pages // 2

    @pl.when(b == 0)
    def _():
        for cp in _page_copies(pt_ref, 0, pool_ref, kbuf.at[0], sem.at[0], n_pages):
            cp.start()
        ps_ref[...] = jnp.zeros(ps_ref.shape, F32)

    @pl.when(b + 1 < n_seq)
    def _():
        for cp in _page_copies(pt_ref, b + 1, pool_ref, kbuf.at[1 - slot], sem.at[1 - slot], n_pages):
            cp.start()

    lane = lax.broadcasted_iota(jnp.int32, (1, LANES), 1)
    mine = lane == b
    qcol = jnp.sum(jnp.where(mine, qt_ref[...], 0.0), axis=1, keepdims=True)
    qcol_ref[...] = jnp.broadcast_to(qcol, qcol_ref.shape)
    qk = jnp.sum(jnp.where(mine, qt_ref[...] * ktn_ref[...], 0.0), axis=1, keepdims=True)

    for cp in _page_copies(pt_ref, b, pool_ref, kbuf.at[slot], sem.at[slot], n_pages):
        cp.wait()

    sub = lax.broadcasted_iota(jnp.int32, (HEADS, LANES), 0)

    def page_scores(pg):
        tot = jnp.zeros((HEADS, LANES), F32)
        for h in range(HEADS):
            th = jnp.sum(kbuf[slot, pg, h] * qcol_ref[h * HD:(h + 1) * HD, :], axis=0, keepdims=True)
            tot = jnp.where(sub == h, th, tot)
        return tot

    def blk_body(n, carry):
        s_ref[n, :, 0:PAGE] = page_scores(2 * n)
        s_ref[n, :, PAGE:2 * PAGE] = page_scores(2 * n + 1)
        return carry

    lax.fori_loop(0, nblk, blk_body, 0)

    sub1 = lax.broadcasted_iota(jnp.int32, (HEADS, 1), 0)
    s_self = jnp.zeros((HEADS, 1), F32)
    for h in range(HEADS):
        s_self = jnp.where(sub1 == h, jnp.sum(qk[h * HD:(h + 1) * HD], axis=0, keepdims=True), s_self)
    s_self = s_self * MOBA_SCALE

    s3 = s_ref[...]
    gate = jnp.sum(s3, axis=2, keepdims=True)
    nidx = lax.broadcasted_iota(jnp.int32, gate.shape, 0)
    cnt = jnp.zeros(gate.shape, jnp.int32)
    for m in range(nblk):
        gm = gate[m:m + 1]
        cnt = cnt + jnp.where((gm > gate) | ((gm == gate) & (m < nidx)), 1, 0)
    sel = cnt < MOBA_TOPK
    lane8 = lax.broadcasted_iota(jnp.int32, (HEADS, LANES), 1)
    idx_tile = jnp.zeros((HEADS, LANES), jnp.int32)
    for r in range(MOBA_TOPK):
        idx_tile = jnp.where(lane8 == r, jnp.sum(jnp.where(cnt == r, nidx, 0), axis=0), idx_tile)
    idx_ref[0] = idx_tile

    sc = jnp.where(sel, s3 * MOBA_SCALE, NEG)
    mx = jnp.max(jnp.max(sc, axis=0), axis=1, keepdims=True)
    mx = jnp.maximum(mx, s_self)
    p3 = jnp.exp(sc - mx)
    p_self = jnp.exp(s_self - mx)
    l = jnp.sum(jnp.sum(p3, axis=0), axis=1, keepdims=True) + p_self
    inv = 1.0 / l
    p_ref[0] = p3 * inv
    p_self = p_self * inv
    for h in range(HEADS):
        rows = slice(h * HD, (h + 1) * HD)
        ps_ref[rows, :] = ps_ref[rows, :] + jnp.where(mine, jnp.broadcast_to(p_self[h:h + 1, :], (HD, LANES)), 0.0)


def _moba_sample_scores(page_table, qt, ktn, pool_kt):
    n_seq, n_pages = page_table.shape
    nblk = n_pages // 2
    full = lambda b, pt: (0, 0)
    grid_spec = pltpu.PrefetchScalarGridSpec(
        num_scalar_prefetch=1, grid=(n_seq,),
        in_specs=[pl.BlockSpec(qt.shape, full), pl.BlockSpec(ktn.shape, full), pl.BlockSpec(memory_space=pl.ANY)],
        out_specs=[pl.BlockSpec((1, nblk, HEADS, MOBA_BLOCK), lambda b, pt: (b, 0, 0, 0)),
                   pl.BlockSpec((1, HEADS, LANES), lambda b, pt: (b, 0, 0)),
                   pl.BlockSpec((WIDTH, n_seq), full)],
        scratch_shapes=[pltpu.VMEM((2, n_pages, HEADS, HD, PAGE), F32), pltpu.SemaphoreType.DMA((2,)),
                        pltpu.VMEM((WIDTH, LANES), F32), pltpu.VMEM((nblk, HEADS, MOBA_BLOCK), F32)])
    return pl.pallas_call(
        functools.partial(_moba_sample_score_kernel, n_pages=n_pages, n_seq=n_seq), grid_spec=grid_spec,
        out_shape=[jax.ShapeDtypeStruct((n_seq, nblk, HEADS, MOBA_BLOCK), F32),
                   jax.ShapeDtypeStruct((n_seq, HEADS, LANES), jnp.int32),
                   jax.ShapeDtypeStruct((WIDTH, n_seq), F32)],
        compiler_params=_cparams(1), name="moba_sample_scores")(page_table, qt, ktn, pool_kt)


def _value_copies(pt_ref, idx_ref, seq, pool_ref, buf_ref, sem):
    cps = []
    for h in range(HEADS):
        for r in range(MOBA_TOPK):
            blk = idx_ref[seq, h * MOBA_TOPK + r]
            for half in range(2):
                page = pt_ref[seq, 2 * blk + half]
                cps.append(pltpu.make_async_copy(pool_ref.at[page, h], buf_ref.at[h, r, half], sem))
    return cps


def _moba_sample_value_kernel(pt_ref, idx_ref, p_ref, ps_ref, vtn_ref, pool_ref, o_ref, vbuf, sem, *, n_seq):
    b = pl.program_id(0)
    slot = b % 2

    @pl.when(b == 0)
    def _():
        for cp in _value_copies(pt_ref, idx_ref, 0, pool_ref, vbuf.at[0], sem.at[0]):
            cp.start()
        o_ref[...] = jnp.zeros(o_ref.shape, F32)

    @pl.when(b + 1 < n_seq)
    def _():
        for cp in _value_copies(pt_ref, idx_ref, b + 1, pool_ref, vbuf.at[1 - slot], sem.at[1 - slot]):
            cp.start()

    for cp in _value_copies(pt_ref, idx_ref, b, pool_ref, vbuf.at[slot], sem.at[slot]):
        cp.wait()

    mine = lax.broadcasted_iota(jnp.int32, (1, LANES), 1) == b
    for h in range(HEADS):
        acc = jnp.zeros((HD, PAGE), F32)
        for r in range(MOBA_TOPK):
            blk = idx_ref[b, h * MOBA_TOPK + r]
            prow = p_ref[0, pl.ds(blk, 1), pl.ds(h, 1), :].reshape(1, MOBA_BLOCK)
            for half in range(2):
                acc = acc + prow[:, half * PAGE:(half + 1) * PAGE] * vbuf[slot, h, r, half]
        col = jnp.sum(acc, axis=1, keepdims=True)
        rows = slice(h * HD, (h + 1) * HD)
        o_ref[rows, :] = o_ref[rows, :] + jnp.where(mine, col, 0.0)

    @pl.when(b == n_seq - 1)
    def _():
        o_ref[...] = o_ref[...] + ps_ref[...] * vtn_ref[...]


def _moba_sample_values(page_table, idx, p4, ps_t, vtn, pool_vt):
    n_seq, n_pages = page_table.shape
    nblk = n_pages // 2
    full = lambda b, pt, ix: (0, 0)
    grid_spec = pltpu.PrefetchScalarGridSpec(
        num_scalar_prefetch=2, grid=(n_seq,),
        in_specs=[pl.BlockSpec((1, nblk, HEADS, MOBA_BLOCK), lambda b, pt, ix: (b, 0, 0, 0)),
                  pl.BlockSpec(ps_t.shape, full), pl.BlockSpec(vtn.shape, full), pl.BlockSpec(memory_space=pl.ANY)],
        out_specs=pl.BlockSpec((WIDTH, n_seq), full),
        scratch_shapes=[pltpu.VMEM((2, HEADS, MOBA_TOPK, 2, HD, PAGE), F32), pltpu.SemaphoreType.DMA((2,))])
    return pl.pallas_call(
        functools.partial(_moba_sample_value_kernel, n_seq=n_seq), grid_spec=grid_spec,
        out_shape=jax.ShapeDtypeStruct((WIDTH, n_seq), F32),
        compiler_params=_cparams(1), name="moba_sample_values")(page_table, idx, p4, ps_t, vtn, pool_vt)


def _mla_sample_kernel(pt_ref, ql_ref, qr_ref, cn_ref, kn_ref, cpool_ref, rpool_ref, o_ref,
                       cbuf, rbuf, sem, s_ref, *, n_pages, n_seq, chunk_pages):
    b = pl.program_id(0)
    slot = b % 2
    n_chunks = n_pages // chunk_pages
    ct = chunk_pages * PAGE

    def copies(seq, sl):
        cps = []
        for pg in range(n_pages):
            page = pt_ref[seq, pg]
            cps.append(pltpu.make_async_copy(cpool_ref.at[page], cbuf.at[sl, pl.ds(pg * PAGE, PAGE)], sem.at[sl]))
            cps.append(pltpu.make_async_copy(rpool_ref.at[page], rbuf.at[sl, pg], sem.at[sl]))
        return cps

    @pl.when(b == 0)
    def _():
        for cp in copies(0, 0):
            cp.start()

    @pl.when(b + 1 < n_seq)
    def _():
        for cp in copies(b + 1, 1 - slot):
            cp.start()

    ql = ql_ref[0]
    qr = qr_ref[0]
    qlb = ql.astype(BF16)
    qrb = qr.astype(BF16)
    s_self = (jnp.sum(ql * cn_ref[0], axis=1, keepdims=True) + jnp.sum(qr * kn_ref[0], axis=1, keepdims=True))

    for cp in copies(b, slot):
        cp.wait()

    def score_body(c, carry):
        start = pl.multiple_of(c * ct, ct)
        cb = cbuf[slot, pl.ds(start, ct), :].astype(BF16)
        parts = [_dot(qrb, rbuf[slot, c * chunk_pages + j].astype(BF16)) for j in range(chunk_pages)]
        s_ref[c] = _dot_nt(qlb, cb) + jnp.concatenate(parts, axis=1)
        return carry

    lax.fori_loop(0, n_chunks, score_body, 0)

    s3 = s_ref[...]
    mx = jnp.maximum(jnp.max(jnp.max(s3, axis=0), axis=1, keepdims=True), s_self)
    p3 = jnp.exp(s3 - mx)
    p_self = jnp.exp(s_self - mx)
    l = jnp.sum(jnp.sum(p3, axis=0), axis=1, keepdims=True) + p_self
    s_ref[...] = p3

    def pv_body(c, acc):
        start = pl.multiple_of(c * ct, ct)
        cb = cbuf[slot, pl.ds(start, ct), :].astype(BF16)
        return acc + _dot(s_ref[c].astype(BF16), cb)

    acc = lax.fori_loop(0, n_chunks, pv_body, jnp.zeros((HEADS, MLA_KV_RANK), F32))
    o_ref[0] = (acc + p_self * cn_ref[0]) / l


def _mla_sample(page_table, ql, qr, cn, kn, pool_c, pool_rt, chunk_pages=4):
    n_seq, n_pages = page_table.shape
    blk = lambda b, pt: (b, 0, 0)
    n_chunks = n_pages // chunk_pages
    grid_spec = pltpu.PrefetchScalarGridSpec(
        num_scalar_prefetch=1, grid=(n_seq,),
        in_specs=[pl.BlockSpec((1, HEADS, MLA_KV_RANK), blk), pl.BlockSpec((1, HEADS, MLA_ROPE), blk),
                  pl.BlockSpec((1, 1, MLA_KV_RANK), blk), pl.BlockSpec((1, 1, MLA_ROPE), blk),
                  pl.BlockSpec(memory_space=pl.ANY), pl.BlockSpec(memory_space=pl.ANY)],
        out_specs=pl.BlockSpec((1, HEADS, MLA_KV_RANK), blk),
        scratch_shapes=[pltpu.VMEM((2, n_pages * PAGE, MLA_KV_RANK), F32),
                        pltpu.VMEM((2, n_pages, MLA_ROPE, PAGE), F32),
                        pltpu.SemaphoreType.DMA((2,)),
                        pltpu.VMEM((n_chunks, HEADS, chunk_pages * PAGE), F32)])
    return pl.pallas_call(
        functools.partial(_mla_sample_kernel, n_pages=n_pages, n_seq=n_seq, chunk_pages=chunk_pages),
        grid_spec=grid_spec, out_shape=jax.ShapeDtypeStruct((n_seq, HEADS, MLA_KV_RANK), F32),
        compiler_params=_cparams(1), name="mla_sample")(page_table, ql, qr, cn, kn, pool_c, pool_rt)


def _rope_tables(pos, theta, n_rot, period, offset):
    half = n_rot // 2
    inv = theta ** (-jnp.arange(half, dtype=F32) * 2.0 / n_rot)
    ang = pos.astype(F32)[:, None] * inv[None, :]
    cos, sin = jnp.cos(ang), jnp.sin(ang)
    n = pos.shape[0]
    one, zero = jnp.ones((n, 1), F32), jnp.zeros((n, 1), F32)
    unit_c = jnp.concatenate([jnp.broadcast_to(one, (n, offset)), cos, cos,
                              jnp.broadcast_to(one, (n, period - offset - n_rot))], axis=1)
    unit_a = jnp.concatenate([jnp.broadcast_to(zero, (n, offset)), -sin, jnp.broadcast_to(zero, (n, half)),
                              jnp.broadcast_to(zero, (n, period - offset - n_rot))], axis=1)
    unit_b = jnp.concatenate([jnp.broadcast_to(zero, (n, offset)), jnp.broadcast_to(zero, (n, half)), sin,
                              jnp.broadcast_to(zero, (n, period - offset - n_rot))], axis=1)
    rep = LANES // period
    return tuple(jnp.tile(u, (1, rep)) for u in (unit_c, unit_a, unit_b))


def _prep_weights(w_in, mla_w_uq, mla_w_uk, mla_w_uv):
    w = w_in
    q, k, v = w[:, 0:512], w[:, 512:1024], w[:, 1024:1536]
    cq, ckv = w[:, 1536:1920], w[:, 1920:2176]
    kr, gates = w[:, 2176:2208], w[:, 2208:]
    z = lambda n: jnp.zeros((D_MODEL, n), w.dtype)
    w_mix = jnp.concatenate([q, k, v, cq, ckv, z(HD), kr, z(LANES - HD - MLA_ROPE)], axis=1).astype(BF16)
    w_gates = gates.astype(BF16)

    per = HD + MLA_ROPE
    uq = mla_w_uq.reshape(MLA_Q_RANK, HEADS, per)
    wuq = jnp.concatenate([uq, jnp.zeros((MLA_Q_RANK, HEADS, LANES - per), uq.dtype)], axis=2)
    wuq = wuq.reshape(MLA_Q_RANK, HEADS * LANES).astype(BF16)

    uk_pad = jnp.concatenate([mla_w_uk, jnp.zeros((MLA_KV_RANK, HEADS, LANES - HD), mla_w_uk.dtype)], axis=2)
    wuk = uk_pad.reshape(MLA_KV_RANK, HEADS * LANES).astype(BF16)
    wuv = mla_w_uv.reshape(MLA_KV_RANK, WIDTH).astype(BF16)

    eye = jnp.eye(HEADS, dtype=F32)
    ukt = jnp.transpose(mla_w_uk, (1, 2, 0))
    ukt = jnp.concatenate([ukt, jnp.zeros((HEADS, LANES - HD, MLA_KV_RANK), F32)], axis=1)
    wqlat = (ukt[:, :, None, :] * eye[:, None, :, None]).reshape(HEADS * LANES, HEADS * MLA_KV_RANK).astype(BF16)
    lane_r = jnp.zeros((LANES, MLA_ROPE), F32).at[HD + jnp.arange(MLA_ROPE), jnp.arange(MLA_ROPE)].set(1.0)
    pr = (lane_r[None, :, None, :] * eye[:, None, :, None]).reshape(HEADS * LANES, HEADS * MLA_ROPE).astype(BF16)
    pk = jnp.concatenate([lane_r, jnp.zeros((LANES, LANES - MLA_ROPE), F32)], axis=1).astype(BF16)
    uvt = jnp.transpose(mla_w_uv, (1, 0, 2))
    wuvbd = (uvt[:, :, None, :] * eye[:, None, :, None]).reshape(HEADS * MLA_KV_RANK, WIDTH).astype(BF16)
    return w_mix, w_gates, wuq, wuk, wuv, wqlat, pr, pk, wuvbd


def kernel(x_prompt, x_sample, cache_moba_k, cache_moba_v, cache_mla_ckv, cache_mla_krope, cache_mem_k, cache_mem_v, page_table, mem_prompt, ffn1_norm, ffn1_w_gate, ffn1_w_up, ffn1_w_down, mix_norm, w_in, mla_q_norm, mla_w_uq, mla_kv_norm, mla_w_uk, mla_w_uv, moba_w_o, mla_w_o, w_out, xattn_norm, mem_norm, xattn_w_q, xattn_w_k, xattn_w_v, xattn_w_o, ffn2_norm, ffn2_w_gate, ffn2_w_up, ffn2_w_down, final_norm):
    bsz, seq, _ = x_prompt.shape
    n_seq = x_sample.shape[0]
    n_pages = page_table.shape[1]
    past_len = n_pages * PAGE
    assert x_sample.shape[1] == 1 and cache_moba_k.shape[0] == 1 and n_seq == LANES
    tm = min(512, seq)

    bf = lambda a: a[0].astype(BF16)
    w_mix, w_gates, wuq, wuk, wuv, wqlat, pr, pk, wuvbd = _prep_weights(w_in[0], mla_w_uq[0], mla_w_uk[0], mla_w_uv[0])
    f1 = (ffn1_norm, bf(ffn1_w_gate), bf(ffn1_w_up), bf(ffn1_w_down))
    f2 = (ffn2_norm, bf(ffn2_w_gate), bf(ffn2_w_up), bf(ffn2_w_down))
    fn = final_norm.reshape(1, D_MODEL)
    merge_w = (mix_norm, w_gates, bf(moba_w_o), bf(mla_w_o), bf(w_out), xattn_norm, bf(xattn_w_q))
    xwo = bf(xattn_w_o)

    pos_p = jnp.arange(seq, dtype=jnp.int32)
    pos_s = jnp.full((1,), past_len, jnp.int32)
    tabs_p = _rope_tables(pos_p, MOBA_THETA, MOBA_ROT, HD, 0) + _rope_tables(pos_p, MLA_THETA, MLA_ROPE, LANES, HD)
    tabs_s = tuple(jnp.broadcast_to(t, (n_seq, LANES)) for t in
                   _rope_tables(pos_s, MOBA_THETA, MOBA_ROT, HD, 0) + _rope_tables(pos_s, MLA_THETA, MLA_ROPE, LANES, HD))

    xp = x_prompt.reshape(bsz * seq, D_MODEL)
    x1 = _ffn(xp, *f1, tm=tm, name="ffn1_prompt")
    (qf, kb, kt, vt, vt4, kmean, qcat, kcat, vtm4, ckv_p, krt) = _mix_prompt(
        x1, mix_norm, w_mix, mla_q_norm, wuq, mla_kv_norm, wuk, wuv, tabs_p, bsz=bsz, seq=seq, tm=tm)
    nq = seq // MOBA_BLOCK
    oa_t = _attn_prompt(qf, kb, vt4, kmean.reshape(bsz, nq, WIDTH), bsz=bsz, seq=seq, moba=True)
    ob_t = _attn_prompt(qcat, kcat, vtm4, None, bsz=bsz, seq=seq, moba=False)
    x2, qx = _merge(x1.reshape(bsz, seq, D_MODEL), oa_t, ob_t, None, *merge_w, tm=tm, name="merge_prompt")
    mk, mv, mkb, mvb = _memkv(mem_prompt, mem_norm, bf(xattn_w_k), bf(xattn_w_v))
    ox = _xattn_prompt(qx, mkb, mvb, tm=tm)
    y_prompt = _ffn(x2.reshape(bsz * seq, D_MODEL), *f2, tm=tm, pre=(ox.reshape(bsz * seq, WIDTH), xwo), fnorm=fn,
                    name="ffn2_prompt").reshape(bsz, seq, D_MODEL)

    p_moba_k = jnp.transpose(kt.reshape(bsz, HEADS, HD, seq), (0, 3, 1, 2))[None]
    p_moba_v = jnp.transpose(vt.reshape(bsz, HEADS, HD, seq), (0, 3, 1, 2))[None]
    p_mla_ckv = ckv_p.reshape(1, bsz, seq, MLA_KV_RANK)
    p_mla_krope = jnp.transpose(krt, (0, 2, 1))[None]
    n_mem = mem_prompt.shape[1]
    p_mem_k = mk.reshape(1, bsz, n_mem, XHEADS, XHD)
    p_mem_v = mv.reshape(1, bsz, n_mem, XHEADS, XHD)

    xs = x_sample.reshape(n_seq, D_MODEL)
    xs1 = _ffn(xs, *f1, tm=n_seq, name="ffn1_sample")
    qt, ktn, vtn, ckv_s, krt_s, ql, qr, krow = _mix_sample(
        xs1, mix_norm, w_mix, mla_q_norm, wuq, mla_kv_norm, wqlat, pr, pk, tabs_s)
    pool_kt = jnp.transpose(cache_moba_k[0], (0, 2, 3, 1))
    pool_vt = jnp.transpose(cache_moba_v[0], (0, 2, 3, 1))
    pool_rt = jnp.transpose(cache_mla_krope[0], (0, 2, 1))
    p4, idx, ps_t = _moba_sample_scores(page_table, qt, ktn, pool_kt)
    idx2 = idx[:, :, :MOBA_TOPK].reshape(n_seq, HEADS * MOBA_TOPK)
    oa_s = _moba_sample_values(page_table, idx2, p4, ps_t, vtn, pool_vt)
    olat = _mla_sample(page_table, ql.reshape(n_seq, HEADS, MLA_KV_RANK), qr.reshape(n_seq, HEADS, MLA_ROPE),
                       ckv_s.reshape(n_seq, 1, MLA_KV_RANK), krow[:, :MLA_ROPE].reshape(n_seq, 1, MLA_ROPE),
                       cache_mla_ckv[0], pool_rt)
    xs2, qxs = _merge(xs1.reshape(1, n_seq, D_MODEL), oa_s.reshape(1, WIDTH, n_seq),
                      olat.reshape(1, n_seq, HEADS * MLA_KV_RANK), wuvbd, *merge_w, tm=n_seq, name="merge_sample")
    n_mem_s = cache_mem_k.shape[2]
    memk = cache_mem_k[0].reshape(n_seq, n_mem_s * XHEADS, XHD)
    memv = cache_mem_v[0].reshape(n_seq, n_mem_s * XHEADS, XHD)
    oxs = _xattn_sample(qxs.reshape(n_seq, XHEADS, XHD), memk, memv)
    y_sample = _ffn(xs2.reshape(n_seq, D_MODEL), *f2, tm=n_seq, pre=(oxs.reshape(n_seq, WIDTH).astype(BF16), xwo),
                    fnorm=fn, name="ffn2_sample").reshape(n_seq, 1, D_MODEL)

    s_moba_k = jnp.transpose(ktn.reshape(HEADS, HD, n_seq), (2, 0, 1)).reshape(1, n_seq, 1, HEADS, HD)
    s_moba_v = jnp.transpose(vtn.reshape(HEADS, HD, n_seq), (2, 0, 1)).reshape(1, n_seq, 1, HEADS, HD)
    s_mla_ckv = ckv_s.reshape(1, n_seq, 1, MLA_KV_RANK)
    s_mla_krope = jnp.transpose(krt_s, (1, 0)).reshape(1, n_seq, 1, MLA_ROPE)

    return (y_prompt, y_sample, p_moba_k, p_moba_v, p_mla_ckv, p_mla_krope, p_mem_k, p_mem_v,
            s_moba_k, s_moba_v, s_mla_ckv, s_mla_krope)
```

```python
import functools
import math

import jax
import jax.numpy as jnp
from jax import lax
from jax.experimental import pallas as pl
from jax.experimental.pallas import tpu as pltpu

F32 = jnp.float32
BF16 = jnp.bfloat16

D_MODEL = 1024
PAGE = 128
HEADS = 8
HD = 64
WIDTH = HEADS * HD
MOBA_BLOCK = 256
MOBA_TOPK = 3
MOBA_ROT = HD // 4
MOBA_THETA = 500000.0
MLA_ROPE = 32
MLA_THETA = 10000.0
MLA_Q_RANK = 384
MLA_KV_RANK = 256
XHEADS = 4
XHD = 128
D_FF = 2816
RMS_EPS = 1e-6
LANES = 128
NEG = -1e30
VMEM_LIMIT = 56 * 1024 * 1024

MOBA_SCALE = 1.0 / math.sqrt(HD)
MLA_SCALE = 1.0 / math.sqrt(HD + MLA_ROPE)
X_SCALE = 1.0 / math.sqrt(XHD)
LOG2E = math.log2(math.e)

C_Q, C_K, C_V, C_CQ, C_CKV, C_KR, C_END = 0, 512, 1024, 1536, 1920, 2176, 2304


def _cparams(n_axes=1, vmem=VMEM_LIMIT):
    return pltpu.CompilerParams(dimension_semantics=("arbitrary",) * n_axes, vmem_limit_bytes=vmem)


def _const_spec(shape):
    nd = len(shape)
    return pl.BlockSpec(shape, lambda *_: (0,) * nd, pipeline_mode=pl.Buffered(1))


def _dot(a, b):
    return jnp.dot(a, b, preferred_element_type=F32)


def _dot_nt(a, b):
    return lax.dot_general(a, b, (((1,), (1,)), ((), ())), preferred_element_type=F32)


def _split3(a):
    hi = a.astype(BF16)
    r1 = a - hi.astype(F32)
    mid = r1.astype(BF16)
    lo = (r1 - mid.astype(F32)).astype(BF16)
    return hi, mid, lo


def _dot_nt_f32(a, b):
    ah, am, al = _split3(a)
    bh, bm, bl = _split3(b)
    return (_dot_nt(ah, bh) + (_dot_nt(ah, bm) + _dot_nt(am, bh))
            + (_dot_nt(ah, bl) + _dot_nt(am, bm) + _dot_nt(al, bh)))


def _rms(x, g):
    ms = jnp.mean(x * x, axis=-1, keepdims=True)
    return x * lax.rsqrt(ms + RMS_EPS) * g


def _sigmoid(x):
    return 1.0 / (1.0 + jnp.exp(-x))


def _rope_block(x, c, a, b, half):
    up = pltpu.roll(x, LANES - half, 1)
    dn = pltpu.roll(x, half, 1)
    return x * c + up * a + dn * b


def _rope_wide(x, tabs, half):
    c, a, b = tabs
    blocks = [_rope_block(x[:, j:j + LANES], c, a, b, half) for j in range(0, x.shape[1], LANES)]
    return blocks[0] if len(blocks) == 1 else jnp.concatenate(blocks, axis=1)


def _ffn_kernel(*refs, pre_proj, final_norm, n_chunks):
    it = iter(refs)
    x_ref = next(it)
    if pre_proj:
        o_ref, wo_ref = next(it), next(it)
    g_ref, wg_ref, wu_ref, wd_ref = next(it), next(it), next(it), next(it)
    fn_ref = next(it) if final_norm else None
    out_ref = next(it)

    x = x_ref[...]
    if pre_proj:
        x = x + _dot(o_ref[...], wo_ref[...])
    h = _rms(x, g_ref[...]).astype(BF16)
    fc = D_FF // n_chunks
    acc = jnp.zeros(x.shape, F32)
    for c in range(n_chunks):
        gt = _dot(h, wg_ref[:, c * fc:(c + 1) * fc])
        up = _dot(h, wu_ref[:, c * fc:(c + 1) * fc])
        act = (gt * _sigmoid(gt) * up).astype(BF16)
        acc = acc + _dot(act, wd_ref[c * fc:(c + 1) * fc, :])
    y = x + 0.5 * acc
    if final_norm:
        y = _rms(y, fn_ref[...])
    out_ref[...] = y


def _ffn(x, g, wg, wu, wd, *, tm, pre=None, fnorm=None, name):
    t = x.shape[0]
    args, specs = [x], [pl.BlockSpec((tm, D_MODEL), lambda i: (i, 0))]
    if pre is not None:
        o, wo = pre
        args += [o, wo]
        specs += [pl.BlockSpec((tm, o.shape[1]), lambda i: (i, 0)), _const_spec(wo.shape)]
    args += [g, wg, wu, wd]
    specs += [_const_spec(g.shape), _const_spec(wg.shape), _const_spec(wu.shape), _const_spec(wd.shape)]
    if fnorm is not None:
        args.append(fnorm)
        specs.append(_const_spec(fnorm.shape))
    kern = functools.partial(_ffn_kernel, pre_proj=pre is not None, final_norm=fnorm is not None, n_chunks=2)
    return pl.pallas_call(
        kern, grid=(t // tm,), in_specs=specs,
        out_specs=pl.BlockSpec((tm, D_MODEL), lambda i: (i, 0)),
        out_shape=jax.ShapeDtypeStruct((t, D_MODEL), F32),
        compiler_params=_cparams(1), name=name)(*args)


def _mix_prompt_kernel(x_ref, g_ref, w_ref, qn_ref, wuq_ref, kvn_ref, wuk_ref, wuv_ref,
                       mc_ref, ma_ref, mb_ref, lc_ref, la_ref, lb_ref,
                       qf_ref, kb_ref, kt_ref, vt_ref, vt4_ref, km_ref,
                       qcat_ref, kcat_ref, vtm4_ref, ckv_ref, krt_ref, *, tm):
    h = _rms(x_ref[...], g_ref[...]).astype(BF16)
    proj = _dot(h, w_ref[...])
    mtab = (mc_ref[...], ma_ref[...], mb_ref[...])
    ltab = (lc_ref[...], la_ref[...], lb_ref[...])
    nblk = tm // MOBA_BLOCK

    q = _rope_wide(proj[:, C_Q:C_K], mtab, MOBA_ROT // 2)
    k = _rope_wide(proj[:, C_K:C_V], mtab, MOBA_ROT // 2)
    v = proj[:, C_V:C_CQ]
    qf_ref[...] = q
    kb_ref[...] = k.astype(BF16)
    kt_ref[0] = k.T
    vt = v.T
    vt_ref[0] = vt
    for j in range(nblk):
        vt4_ref[0, j] = vt[:, j * MOBA_BLOCK:(j + 1) * MOBA_BLOCK].astype(BF16)
        km_ref[0, j:j + 1, :] = jnp.sum(k[j * MOBA_BLOCK:(j + 1) * MOBA_BLOCK], axis=0, keepdims=True) * (1.0 / MOBA_BLOCK)

    cqn = _rms(proj[:, C_CQ:C_CKV], qn_ref[...]).astype(BF16)
    qh = _rope_wide(_dot(cqn, wuq_ref[...]), ltab, MLA_ROPE // 2)
    qcat_ref[...] = (qh * (MLA_SCALE * LOG2E)).astype(BF16)

    ckvn = _rms(proj[:, C_CKV:C_KR], kvn_ref[...])
    ckv_ref[...] = ckvn
    kr = _rope_block(proj[:, C_KR:C_END], *ltab, MLA_ROPE // 2)
    krt_ref[0] = kr.T[HD:HD + MLA_ROPE, :]
    cb = ckvn.astype(BF16)
    kn = _dot(cb, wuk_ref[...])
    kcat_ref[...] = (kn + jnp.concatenate([kr] * HEADS, axis=1)).astype(BF16)
    vtm = _dot(cb, wuv_ref[...]).T
    for j in range(nblk):
        vtm4_ref[0, j] = vtm[:, j * MOBA_BLOCK:(j + 1) * MOBA_BLOCK].astype(BF16)


def _mix_prompt(x, g, w, qn, wuq, kvn, wuk, wuv, tabs, *, bsz, seq, tm):
    t = bsz * seq
    spt = seq // tm
    nblk = tm // MOBA_BLOCK
    nb = seq // MOBA_BLOCK
    row = lambda i: (i, 0)
    tab_spec = pl.BlockSpec((tm, LANES), lambda i: (i % spt, 0))
    in_specs = [pl.BlockSpec((tm, D_MODEL), row), _const_spec(g.shape), _const_spec(w.shape),
                _const_spec(qn.shape), _const_spec(wuq.shape), _const_spec(kvn.shape),
                _const_spec(wuk.shape), _const_spec(wuv.shape)] + [tab_spec] * 6
    tcol = lambda i: (i // spt, 0, i % spt)
    t4 = lambda i: (i // spt, i % spt, 0, 0)
    out_shape = [
        jax.ShapeDtypeStruct((t, WIDTH), F32),
        jax.ShapeDtypeStruct((t, WIDTH), BF16),
        jax.ShapeDtypeStruct((bsz, WIDTH, seq), F32),
        jax.ShapeDtypeStruct((bsz, WIDTH, seq), F32),
        jax.ShapeDtypeStruct((bsz, nb, WIDTH, MOBA_BLOCK), BF16),
        jax.ShapeDtypeStruct((t // tm, nblk, WIDTH), F32),
        jax.ShapeDtypeStruct((t, HEADS * LANES), BF16),
        jax.ShapeDtypeStruct((t, HEADS * LANES), BF16),
        jax.ShapeDtypeStruct((bsz, nb, WIDTH, MOBA_BLOCK), BF16),
        jax.ShapeDtypeStruct((t, MLA_KV_RANK), F32),
        jax.ShapeDtypeStruct((bsz, MLA_ROPE, seq), F32),
    ]
    out_specs = [
        pl.BlockSpec((tm, WIDTH), row), pl.BlockSpec((tm, WIDTH), row),
        pl.BlockSpec((1, WIDTH, tm), tcol), pl.BlockSpec((1, WIDTH, tm), tcol),
        pl.BlockSpec((1, nblk, WIDTH, MOBA_BLOCK), t4),
        pl.BlockSpec((1, nblk, WIDTH), lambda i: (i, 0, 0)),
        pl.BlockSpec((tm, HEADS * LANES), row), pl.BlockSpec((tm, HEADS * LANES), row),
        pl.BlockSpec((1, nblk, WIDTH, MOBA_BLOCK), t4),
        pl.BlockSpec((tm, MLA_KV_RANK), row),
        pl.BlockSpec((1, MLA_ROPE, tm), tcol),
    ]
    return pl.pallas_call(
        functools.partial(_mix_prompt_kernel, tm=tm), grid=(t // tm,), in_specs=in_specs,
        out_specs=out_specs, out_shape=out_shape, compiler_params=_cparams(1), name="mix_prompt",
    )(x, g, w, qn, wuq, kvn, wuk, wuv, *tabs)


def _mix_sample_kernel(x_ref, g_ref, w_ref, qn_ref, wuq_ref, kvn_ref, wqlat_ref, pr_ref, pk_ref,
                       mc_ref, ma_ref, mb_ref, lc_ref, la_ref, lb_ref,
                       qt_ref, kt_ref, vt_ref, ckv_ref, krt_ref, ql_ref, qr_ref, krow_ref):
    h = _rms(x_ref[...], g_ref[...]).astype(BF16)
    proj = _dot(h, w_ref[...])
    mtab = (mc_ref[...], ma_ref[...], mb_ref[...])
    ltab = (lc_ref[...], la_ref[...], lb_ref[...])
    qt_ref[...] = _rope_wide(proj[:, C_Q:C_K], mtab, MOBA_ROT // 2).T
    kt_ref[...] = _rope_wide(proj[:, C_K:C_V], mtab, MOBA_ROT // 2).T
    vt_ref[...] = proj[:, C_V:C_CQ].T

    cqn = _rms(proj[:, C_CQ:C_CKV], qn_ref[...]).astype(BF16)
    qh = (_rope_wide(_dot(cqn, wuq_ref[...]), ltab, MLA_ROPE // 2) * MLA_SCALE).astype(BF16)
    ql_ref[...] = _dot(qh, wqlat_ref[...])
    qr_ref[...] = _dot(qh, pr_ref[...])

    ckv_ref[...] = _rms(proj[:, C_CKV:C_KR], kvn_ref[...])
    kr = _rope_block(proj[:, C_KR:C_END], *ltab, MLA_ROPE // 2)
    krt_ref[...] = kr.T[HD:HD + MLA_ROPE, :]
    krow_ref[...] = _dot(kr.astype(BF16), pk_ref[...])


def _mix_sample(x, g, w, qn, wuq, kvn, wqlat, pr, pk, tabs):
    n = x.shape[0]
    args = (x, g, w, qn, wuq, kvn, wqlat, pr, pk) + tuple(tabs)
    out_shape = [
        jax.ShapeDtypeStruct((WIDTH, n), F32), jax.ShapeDtypeStruct((WIDTH, n), F32),
        jax.ShapeDtypeStruct((WIDTH, n), F32), jax.ShapeDtypeStruct((n, MLA_KV_RANK), F32),
        jax.ShapeDtypeStruct((MLA_ROPE, n), F32), jax.ShapeDtypeStruct((n, HEADS * MLA_KV_RANK), F32),
        jax.ShapeDtypeStruct((n, HEADS * MLA_ROPE), F32), jax.ShapeDtypeStruct((n, LANES), F32),
    ]
    return pl.pallas_call(
        _mix_sample_kernel, grid=(1,), in_specs=[_const_spec(a.shape) for a in args],
        out_specs=[_const_spec(s.shape) for s in out_shape], out_shape=out_shape,
        compiler_params=_cparams(1), name="mix_sample")(*args)


def _attn_prompt_kernel(*refs, moba):
    if moba:
        q_ref, k_ref, vt_ref, km_ref, o_ref, m_ref, l_ref, acc_ref, qs_ref, bias_ref = refs
    else:
        q_ref, k_ref, vt_ref, o_ref, m_ref, l_ref, acc_ref = refs
    qi = pl.program_id(1)
    blk = MOBA_BLOCK

    if moba:
        nb = km_ref.shape[1]
        lane = lax.broadcasted_iota(jnp.int32, (1, LANES), 1)
        bidx = lax.broadcasted_iota(jnp.int32, (nb, 1), 0)
        past = bidx < qi
        wins_tie = [jnp.where(m < bidx, 1, 0) for m in range(nb)]
        for h in range(HEADS):
            pair = slice((h // 2) * LANES, (h // 2 + 1) * LANES)
            lo = (h % 2) * HD
            qh = jnp.where((lane >= lo) & (lane < lo + HD), q_ref[:, pair], 0.0)
            gate = jnp.where(past, _dot_nt_f32(km_ref[0, :, pair], qh), -jnp.inf)
            cnt = jnp.zeros(gate.shape, jnp.int32)
            for m in range(nb):
                gm = gate[m:m + 1, :]
                cnt = cnt + jnp.where(gm > gate, 1, jnp.where(gm == gate, wins_tie[m], 0))
            bias_ref[h] = jnp.where(past, jnp.where(cnt < MOBA_TOPK, 0.0, NEG), NEG)
            qs_ref[h] = (qh * (MOBA_SCALE * LOG2E)).astype(BF16)

    def q_of(h):
        return qs_ref[h] if moba else q_ref[:, h * LANES:(h + 1) * LANES]

    def k_of(h, j):
        col = h // 2 if moba else h
        return k_ref[pl.ds(pl.multiple_of(j * blk, blk), blk), col * LANES:(col + 1) * LANES]

    def v_of(h, j):
        return vt_ref[0, j, h * HD:(h + 1) * HD, :]

    heads = range(HEADS)
    rows = [slice(h * HD, (h + 1) * HD) for h in heads]

    kpos = lax.broadcasted_iota(jnp.int32, (blk, blk), 0)
    qpos = lax.broadcasted_iota(jnp.int32, (blk, blk), 1)
    causal = kpos <= qpos
    s0 = [_dot_nt(k_of(h, qi), q_of(h)) for h in heads]
    p0 = []
    for h in heads:
        s = jnp.where(causal, s0[h], NEG)
        m0 = jnp.max(s, axis=0, keepdims=True)
        p = jnp.exp2(s - m0)
        m_ref[h, 0:1, :] = m0
        l_ref[h, 0:1, :] = jnp.sum(p, axis=0, keepdims=True)
        p0.append(p.astype(BF16))
    for h in heads:
        acc_ref[rows[h], :] = _dot(v_of(h, qi), p0[h])

    def body(j, carry):
        sj = [_dot_nt(k_of(h, j), q_of(h)) for h in heads]
        pj, alpha = [], []
        for h in heads:
            s = sj[h] + bias_ref[h, pl.ds(j, 1), :] if moba else sj[h]
            m = m_ref[h, 0:1, :]
            mn = jnp.maximum(m, jnp.max(s, axis=0, keepdims=True))
            a = jnp.exp2(m - mn)
            p = jnp.exp2(s - mn)
            m_ref[h, 0:1, :] = mn
            l_ref[h, 0:1, :] = l_ref[h, 0:1, :] * a + jnp.sum(p, axis=0, keepdims=True)
            pj.append(p.astype(BF16))
            alpha.append(a)
        for h in heads:
            acc_ref[rows[h], :] = acc_ref[rows[h], :] * alpha[h] + _dot(v_of(h, j), pj[h])
        return carry

    lax.fori_loop(0, qi, body, 0)
    for h in heads:
        o_ref[0, rows[h], :] = acc_ref[rows[h], :] / l_ref[h, 0:1, :]


def _attn_prompt(q, k, vt4, kmean, *, bsz, seq, moba):
    nq = seq // MOBA_BLOCK
    qw = q.shape[1]
    args = [q, k, vt4]
    specs = [pl.BlockSpec((MOBA_BLOCK, qw), lambda b, i: (b * nq + i, 0)),
             pl.BlockSpec((seq, qw), lambda b, i: (b, 0)),
             pl.BlockSpec((1, nq, WIDTH, MOBA_BLOCK), lambda b, i: (b, 0, 0, 0))]
    scratch = [pltpu.VMEM((HEADS, 8, MOBA_BLOCK), F32), pltpu.VMEM((HEADS, 8, MOBA_BLOCK), F32),
               pltpu.VMEM((WIDTH, MOBA_BLOCK), F32)]
    if moba:
        args.append(kmean)
        specs.append(pl.BlockSpec((1, nq, WIDTH), lambda b, i: (b, 0, 0)))
        scratch += [pltpu.VMEM((HEADS, MOBA_BLOCK, LANES), BF16), pltpu.VMEM((HEADS, nq, MOBA_BLOCK), F32)]
    return pl.pallas_call(
        functools.partial(_attn_prompt_kernel, moba=moba), grid=(bsz, nq),
        in_specs=specs, out_specs=pl.BlockSpec((1, WIDTH, MOBA_BLOCK), lambda b, i: (b, 0, i)),
        out_shape=jax.ShapeDtypeStruct((bsz, WIDTH, seq), F32), scratch_shapes=scratch,
        compiler_params=_cparams(2), name="attn_moba" if moba else "attn_mla")(*args)


def _merge_kernel(*refs, b_latent):
    it = iter(refs)
    x_ref, oa_ref, ob_ref = next(it), next(it), next(it)
    wuvbd_ref = next(it) if b_latent else None
    g_ref, wg_ref, woa_ref, wob_ref, wout_ref, gx_ref, wq_ref, x2_ref, qx_ref = it

    x = x_ref[0]
    h = _rms(x, g_ref[...]).astype(BF16)
    gates = _dot(h, wg_ref[...])
    oa = _dot(oa_ref[0].T.astype(BF16), woa_ref[...])
    if b_latent:
        ob_in = _dot(ob_ref[0].astype(BF16), wuvbd_ref[...]).astype(BF16)
    else:
        ob_in = ob_ref[0].T.astype(BF16)
    ob = _dot(ob_in, wob_ref[...])
    mix = _sigmoid(gates[:, :D_MODEL]) * oa + _sigmoid(gates[:, D_MODEL:]) * ob
    x2 = x + _dot(mix.astype(BF16), wout_ref[...])
    x2_ref[0] = x2
    hx = _rms(x2, gx_ref[...]).astype(BF16)
    qx_ref[0] = (_dot(hx, wq_ref[...]) * X_SCALE).astype(BF16)


def _merge(x3, oa_t, ob, wuvbd, g, wg, woa, wob, wout, gx, wq, *, tm, name):
    bsz, seq, _ = x3.shape
    b_latent = wuvbd is not None
    tok = lambda b, i: (b, i, 0)
    col = lambda b, i: (b, 0, i)
    args = [x3, oa_t, ob]
    specs = [pl.BlockSpec((1, tm, D_MODEL), tok), pl.BlockSpec((1, WIDTH, tm), col)]
    if b_latent:
        specs.append(pl.BlockSpec((1, tm, ob.shape[2]), tok))
        args.append(wuvbd)
        specs.append(_const_spec(wuvbd.shape))
    else:
        specs.append(pl.BlockSpec((1, WIDTH, tm), col))
    consts = [g, wg, woa, wob, wout, gx, wq]
    args += consts
    specs += [_const_spec(a.shape) for a in consts]
    return pl.pallas_call(
        functools.partial(_merge_kernel, b_latent=b_latent), grid=(bsz, seq // tm), in_specs=specs,
        out_specs=[pl.BlockSpec((1, tm, D_MODEL), tok), pl.BlockSpec((1, tm, WIDTH), tok)],
        out_shape=[jax.ShapeDtypeStruct((bsz, seq, D_MODEL), F32), jax.ShapeDtypeStruct((bsz, seq, WIDTH), BF16)],
        compiler_params=_cparams(2), name=name)(*args)


def _memkv_kernel(m_ref, g_ref, wk_ref, wv_ref, k_ref, v_ref, kb_ref, vb_ref):
    hm = _rms(m_ref[0], g_ref[...]).astype(BF16)
    mk = _dot(hm, wk_ref[...])
    mv = _dot(hm, wv_ref[...])
    k_ref[0] = mk
    v_ref[0] = mv
    kb_ref[0] = mk.astype(BF16)
    vb_ref[0] = mv.astype(BF16)


def _memkv(mem, g, wk, wv):
    bsz, nm, _ = mem.shape
    blk = lambda b: (b, 0, 0)
    w = XHEADS * XHD
    return pl.pallas_call(
        _memkv_kernel, grid=(bsz,),
        in_specs=[pl.BlockSpec((1, nm, D_MODEL), blk), _const_spec(g.shape), _const_spec(wk.shape), _const_spec(wv.shape)],
        out_specs=[pl.BlockSpec((1, nm, w), blk)] * 4,
        out_shape=[jax.ShapeDtypeStruct((bsz, nm, w), F32)] * 2 + [jax.ShapeDtypeStruct((bsz, nm, w), BF16)] * 2,
        compiler_params=_cparams(1), name="mem_kv")(mem, g, wk, wv)


def _xattn_prompt_kernel(q_ref, k_ref, v_ref, o_ref):
    outs = []
    for h in range(XHEADS):
        sl = slice(h * XHD, (h + 1) * XHD)
        s = _dot_nt(q_ref[0, :, sl], k_ref[0, :, sl])
        m = jnp.max(s, axis=-1, keepdims=True)
        p = jnp.exp(s - m)
        l = jnp.sum(p, axis=-1, keepdims=True)
        outs.append(_dot(p.astype(BF16), v_ref[0, :, sl]) / l)
    o_ref[0] = jnp.concatenate(outs, axis=1).astype(BF16)


def _xattn_prompt(qx, mk, mv, *, tm):
    bsz, seq, w = qx.shape
    nm = mk.shape[1]
    tok = lambda b, i: (b, i, 0)
    mem = lambda b, i: (b, 0, 0)
    return pl.pallas_call(
        _xattn_prompt_kernel, grid=(bsz, seq // tm),
        in_specs=[pl.BlockSpec((1, tm, w), tok), pl.BlockSpec((1, nm, w), mem), pl.BlockSpec((1, nm, w), mem)],
        out_specs=pl.BlockSpec((1, tm, w), tok), out_shape=jax.ShapeDtypeStruct((bsz, seq, w), BF16),
        compiler_params=_cparams(2), name="xattn_prompt")(qx, mk, mv)


def _xattn_sample_kernel(q_ref, k_ref, v_ref, o_ref):
    rows = k_ref.shape[1]
    col = lax.broadcasted_iota(jnp.int32, (XHEADS, rows), 1)
    hd = lax.broadcasted_iota(jnp.int32, (XHEADS, rows), 0)
    own = (col % XHEADS) == hd
    for g in range(q_ref.shape[0]):
        s = jnp.where(own, _dot_nt(q_ref[g], k_ref[g].astype(BF16)), NEG)
        m = jnp.max(s, axis=-1, keepdims=True)
        p = jnp.exp(s - m)
        l = jnp.sum(p, axis=-1, keepdims=True)
        o_ref[g] = _dot(p.astype(BF16), v_ref[g].astype(BF16)) / l


def _xattn_sample(qx, memk, memv, group=4):
    n = qx.shape[0]
    rows = memk.shape[1]
    blk = lambda b: (b, 0, 0)
    return pl.pallas_call(
        _xattn_sample_kernel, grid=(n // group,),
        in_specs=[pl.BlockSpec((group, XHEADS, XHD), blk), pl.BlockSpec((group, rows, XHD), blk),
                  pl.BlockSpec((group, rows, XHD), blk)],
        out_specs=pl.BlockSpec((group, XHEADS, XHD), blk), out_shape=jax.ShapeDtypeStruct((n, XHEADS, XHD), F32),
        compiler_params=_cparams(1), name="xattn_sample")(qx, memk, memv)


def _page_copies(pt_ref, seq, pool_ref, buf_ref, sem, n_pages):
    return [pltpu.make_async_copy(pool_ref.at[pt_ref[seq, pg]], buf_ref.at[pg], sem) for pg in range(n_pages)]


def _moba_sample_score_kernel(pt_ref, qt_ref, ktn_ref, pool_ref, p_ref, idx_ref, ps_ref,
                              kbuf, sem, qcol_ref, s_ref, *, n_pages, n_seq):
    b = pl.program_id(0)
    slot = b % 2
    nblk = n_pages // 2

    @pl.when(b == 0)
    def _():
        for cp in _page_copies(pt_ref, 0, pool_ref, kbuf.at[0], sem.at[0], n_pages):
            cp.start()
        ps_ref[...] = jnp.zeros(ps_ref.shape, F32)

    @pl.when(b + 1 < n_seq)
    def _():
        for cp in _page_copies(pt_ref, b + 1, pool_ref, kbuf.at[1 - slot], sem.at[1 - slot], n_pages):
            cp.start()

    lane = lax.broadcasted_iota(jnp.int32, (1, LANES), 1)
    mine = lane == b
    qcol = jnp.sum(jnp.where(mine, qt_ref[...], 0.0), axis=1, keepdims=True)
    qcol_ref[...] = jnp.broadcast_to(qcol, qcol_ref.shape)
    qk = jnp.sum(jnp.where(mine, qt_ref[...] * ktn_ref[...], 0.0), axis=1, keepdims=True)

    for cp in _page_copies(pt_ref, b, pool_ref, kbuf.at[slot], sem.at[slot], n_pages):
        cp.wait()

    sub = lax.broadcasted_iota(jnp.int32, (HEADS, LANES), 0)

    def page_scores(pg):
        tot = jnp.zeros((HEADS, LANES), F32)
        for h in range(HEADS):
            th = jnp.sum(kbuf[slot, pg, h] * qcol_ref[h * HD:(h + 1) * HD, :], axis=0, keepdims=True)
            tot = jnp.where(sub == h, th, tot)
        return tot

    def blk_body(n, carry):
        s_ref[n, :, 0:PAGE] = page_scores(2 * n)
        s_ref[n, :, PAGE:2 * PAGE] = page_scores(2 * n + 1)
        return carry

    lax.fori_loop(0, nblk, blk_body, 0)

    sub1 = lax.broadcasted_iota(jnp.int32, (HEADS, 1), 0)
    s_self = jnp.zeros((HEADS, 1), F32)
    for h in range(HEADS):
        s_self = jnp.where(sub1 == h, jnp.sum(qk[h * HD:(h + 1) * HD], axis=0, keepdims=True), s_self)
    s_self = s_self * MOBA_SCALE

    s3 = s_ref[...]
    gate = jnp.sum(s3, axis=2, keepdims=True)
    nidx = lax.broadcasted_iota(jnp.int32, gate.shape, 0)
    cnt = jnp.zeros(gate.shape, jnp.int32)
    for m in range(nblk):
        gm = gate[m:m + 1]
        cnt = cnt + jnp.where((gm > gate) | ((gm == gate) & (m < nidx)), 1, 0)
    sel = cnt < MOBA_TOPK
    lane8 = lax.broadcasted_iota(jnp.int32, (HEADS, LANES), 1)
    idx_tile = jnp.zeros((HEADS, LANES), jnp.int32)
    for r in range(MOBA_TOPK):
        idx_tile = jnp.where(lane8 == r, jnp.sum(jnp.where(cnt == r, nidx, 0), axis=0), idx_tile)
    idx_ref[0] = idx_tile

    sc = jnp.where(sel, s3 * MOBA_SCALE, NEG)
    mx = jnp.max(jnp.max(sc, axis=0), axis=1, keepdims=True)
    mx = jnp.maximum(mx, s_self)
    p3 = jnp.exp(sc - mx)
    p_self = jnp.exp(s_self - mx)
    l = jnp.sum(jnp.sum(p3, axis=0), axis=1, keepdims=True) + p_self
    inv = 1.0 / l
    p_ref[0] = p3 * inv
    p_self = p_self * inv
    for h in range(HEADS):
        rows = slice(h * HD, (h + 1) * HD)
        ps_ref[rows, :] = ps_ref[rows, :] + jnp.where(mine, jnp.broadcast_to(p_self[h:h + 1, :], (HD, LANES)), 0.0)


def _moba_sample_scores(page_table, qt, ktn, pool_kt):
    n_seq, n_pages = page_table.shape
    nblk = n_pages // 2
    full = lambda b, pt: (0, 0)
    grid_spec = pltpu.PrefetchScalarGridSpec(
        num_scalar_prefetch=1, grid=(n_seq,),
        in_specs=[pl.BlockSpec(qt.shape, full), pl.BlockSpec(ktn.shape, full), pl.BlockSpec(memory_space=pl.ANY)],
        out_specs=[pl.BlockSpec((1, nblk, HEADS, MOBA_BLOCK), lambda b, pt: (b, 0, 0, 0)),
                   pl.BlockSpec((1, HEADS, LANES), lambda b, pt: (b, 0, 0)),
                   pl.BlockSpec((WIDTH, n_seq), full)],
        scratch_shapes=[pltpu.VMEM((2, n_pages, HEADS, HD, PAGE), F32), pltpu.SemaphoreType.DMA((2,)),
                        pltpu.VMEM((WIDTH, LANES), F32), pltpu.VMEM((nblk, HEADS, MOBA_BLOCK), F32)])
    return pl.pallas_call(
        functools.partial(_moba_sample_score_kernel, n_pages=n_pages, n_seq=n_seq), grid_spec=grid_spec,
        out_shape=[jax.ShapeDtypeStruct((n_seq, nblk, HEADS, MOBA_BLOCK), F32),
                   jax.ShapeDtypeStruct((n_seq, HEADS, LANES), jnp.int32),
                   jax.ShapeDtypeStruct((WIDTH, n_seq), F32)],
        compiler_params=_cparams(1), name="moba_sample_scores")(page_table, qt, ktn, pool_kt)


def _value_copies(pt_ref, idx_ref, seq, pool_ref, buf_ref, sem):
    cps = []
    for h in range(HEADS):
        for r in range(MOBA_TOPK):
            blk = idx_ref[seq, h * MOBA_TOPK + r]
            for half in range(2):
                page = pt_ref[seq, 2 * blk + half]
                cps.append(pltpu.make_async_copy(pool_ref.at[page, h], buf_ref.at[h, r, half], sem))
    return cps


def _moba_sample_value_kernel(pt_ref, idx_ref, p_ref, ps_ref, vtn_ref, pool_ref, o_ref, vbuf, sem, *, n_seq):
    b = pl.program_id(0)
    slot = b % 2

    @pl.when(b == 0)
    def _():
        for cp in _value_copies(pt_ref, idx_ref, 0, pool_ref, vbuf.at[0], sem.at[0]):
            cp.start()
        o_ref[...] = jnp.zeros(o_ref.shape, F32)

    @pl.when(b + 1 < n_seq)
    def _():
        for cp in _value_copies(pt_ref, idx_ref, b + 1, pool_ref, vbuf.at[1 - slot], sem.at[1 - slot]):
            cp.start()

    for cp in _value_copies(pt_ref, idx_ref, b, pool_ref, vbuf.at[slot], sem.at[slot]):
        cp.wait()

    mine = lax.broadcasted_iota(jnp.int32, (1, LANES), 1) == b
    for h in range(HEADS):
        acc = jnp.zeros((HD, PAGE), F32)
        for r in range(MOBA_TOPK):
            blk = idx_ref[b, h * MOBA_TOPK + r]
            prow = p_ref[0, pl.ds(blk, 1), pl.ds(h, 1), :].reshape(1, MOBA_BLOCK)
            for half in range(2):
                acc = acc + prow[:, half * PAGE:(half + 1) * PAGE] * vbuf[slot, h, r, half]
        col = jnp.sum(acc, axis=1, keepdims=True)
        rows = slice(h * HD, (h + 1) * HD)
        o_ref[rows, :] = o_ref[rows, :] + jnp.where(mine, col, 0.0)

    @pl.when(b == n_seq - 1)
    def _():
        o_ref[...] = o_ref[...] + ps_ref[...] * vtn_ref[...]


def _moba_sample_values(page_table, idx, p4, ps_t, vtn, pool_vt):
    n_seq, n_pages = page_table.shape
    nblk = n_pages // 2
    full = lambda b, pt, ix: (0, 0)
    grid_spec = pltpu.PrefetchScalarGridSpec(
        num_scalar_prefetch=2, grid=(n_seq,),
        in_specs=[pl.BlockSpec((1, nblk, HEADS, MOBA_BLOCK), lambda b, pt, ix: (b, 0, 0, 0)),
                  pl.BlockSpec(ps_t.shape, full), pl.BlockSpec(vtn.shape, full), pl.BlockSpec(memory_space=pl.ANY)],
        out_specs=pl.BlockSpec((WIDTH, n_seq), full),
        scratch_shapes=[pltpu.VMEM((2, HEADS, MOBA_TOPK, 2, HD, PAGE), F32), pltpu.SemaphoreType.DMA((2,))])
    return pl.pallas_call(
        functools.partial(_moba_sample_value_kernel, n_seq=n_seq), grid_spec=grid_spec,
        out_shape=jax.ShapeDtypeStruct((WIDTH, n_seq), F32),
        compiler_params=_cparams(1), name="moba_sample_values")(page_table, idx, p4, ps_t, vtn, pool_vt)


def _mla_sample_kernel(pt_ref, ql_ref, qr_ref, cn_ref, kn_ref, cpool_ref, rpool_ref, o_ref,
                       cbuf, rbuf, sem, s_ref, *, n_pages, n_seq, chunk_pages):
    b = pl.program_id(0)
    slot = b % 2
    n_chunks = n_pages // chunk_pages
    ct = chunk_pages * PAGE

    def copies(seq, sl):
        cps = []
        for pg in range(n_pages):
            page = pt_ref[seq, pg]
            cps.append(pltpu.make_async_copy(cpool_ref.at[page], cbuf.at[sl, pl.ds(pg * PAGE, PAGE)], sem.at[sl]))
            cps.append(pltpu.make_async_copy(rpool_ref.at[page], rbuf.at[sl, pg], sem.at[sl]))
        return cps

    @pl.when(b == 0)
    def _():
        for cp in copies(0, 0):
            cp.start()

    @pl.when(b + 1 < n_seq)
    def _():
        for cp in copies(b + 1, 1 - slot):
            cp.start()

    ql = ql_ref[0]
    qr = qr_ref[0]
    qlb = ql.astype(BF16)
    qrb = qr.astype(BF16)
    s_self = (jnp.sum(ql * cn_ref[0], axis=1, keepdims=True) + jnp.sum(qr * kn_ref[0], axis=1, keepdims=True))

    for cp in copies(b, slot):
        cp.wait()

    def score_body(c, carry):
        start = pl.multiple_of(c * ct, ct)
        cb = cbuf[slot, pl.ds(start, ct), :].astype(BF16)
        parts = [_dot(qrb, rbuf[slot, c * chunk_pages + j].astype(BF16)) for j in range(chunk_pages)]
        s_ref[c] = _dot_nt(qlb, cb) + jnp.concatenate(parts, axis=1)
        return carry

    lax.fori_loop(0, n_chunks, score_body, 0, unroll=True)

    s3 = s_ref[...]
    mx = jnp.maximum(jnp.max(jnp.max(s3, axis=0), axis=1, keepdims=True), s_self)
    p3 = jnp.exp(s3 - mx)
    p_self = jnp.exp(s_self - mx)
    l = jnp.sum(jnp.sum(p3, axis=0), axis=1, keepdims=True) + p_self
    s_ref[...] = p3

    def pv_body(c, acc):
        start = pl.multiple_of(c * ct, ct)
        cb = cbuf[slot, pl.ds(start, ct), :].astype(BF16)
        return acc + _dot(s_ref[c].astype(BF16), cb)

    acc = lax.fori_loop(0, n_chunks, pv_body, jnp.zeros((HEADS, MLA_KV_RANK), F32), unroll=True)
    o_ref[0] = (acc + p_self * cn_ref[0]) / l


def _mla_sample(page_table, ql, qr, cn, kn, pool_c, pool_rt, chunk_pages=4):
    n_seq, n_pages = page_table.shape
    blk = lambda b, pt: (b, 0, 0)
    n_chunks = n_pages // chunk_pages
    grid_spec = pltpu.PrefetchScalarGridSpec(
        num_scalar_prefetch=1, grid=(n_seq,),
        in_specs=[pl.BlockSpec((1, HEADS, MLA_KV_RANK), blk), pl.BlockSpec((1, HEADS, MLA_ROPE), blk),
                  pl.BlockSpec((1, 1, MLA_KV_RANK), blk), pl.BlockSpec((1, 1, MLA_ROPE), blk),
                  pl.BlockSpec(memory_space=pl.ANY), pl.BlockSpec(memory_space=pl.ANY)],
        out_specs=pl.BlockSpec((1, HEADS, MLA_KV_RANK), blk),
        scratch_shapes=[pltpu.VMEM((2, n_pages * PAGE, MLA_KV_RANK), F32),
                        pltpu.VMEM((2, n_pages, MLA_ROPE, PAGE), F32),
                        pltpu.SemaphoreType.DMA((2,)),
                        pltpu.VMEM((n_chunks, HEADS, chunk_pages * PAGE), F32)])
    return pl.pallas_call(
        functools.partial(_mla_sample_kernel, n_pages=n_pages, n_seq=n_seq, chunk_pages=chunk_pages),
        grid_spec=grid_spec, out_shape=jax.ShapeDtypeStruct((n_seq, HEADS, MLA_KV_RANK), F32),
        compiler_params=_cparams(1), name="mla_sample")(page_table, ql, qr, cn, kn, pool_c, pool_rt)


def _rope_tables(pos, theta, n_rot, period, offset):
    half = n_rot // 2
    inv = theta ** (-jnp.arange(half, dtype=F32) * 2.0 / n_rot)
    ang = pos.astype(F32)[:, None] * inv[None, :]
    cos, sin = jnp.cos(ang), jnp.sin(ang)
    n = pos.shape[0]
    one, zero = jnp.ones((n, 1), F32), jnp.zeros((n, 1), F32)
    unit_c = jnp.concatenate([jnp.broadcast_to(one, (n, offset)), cos, cos,
                              jnp.broadcast_to(one, (n, period - offset - n_rot))], axis=1)
    unit_a = jnp.concatenate([jnp.broadcast_to(zero, (n, offset)), -sin, jnp.broadcast_to(zero, (n, half)),
                              jnp.broadcast_to(zero, (n, period - offset - n_rot))], axis=1)
    unit_b = jnp.concatenate([jnp.broadcast_to(zero, (n, offset)), jnp.broadcast_to(zero, (n, half)), sin,
                              jnp.broadcast_to(zero, (n, period - offset - n_rot))], axis=1)
    rep = LANES // period
    return tuple(jnp.tile(u, (1, rep)) for u in (unit_c, unit_a, unit_b))


def _prep_weights(w_in, mla_w_uq, mla_w_uk, mla_w_uv):
    w = w_in
    q, k, v = w[:, 0:512], w[:, 512:1024], w[:, 1024:1536]
    cq, ckv = w[:, 1536:1920], w[:, 1920:2176]
    kr, gates = w[:, 2176:2208], w[:, 2208:]
    z = lambda n: jnp.zeros((D_MODEL, n), w.dtype)
    w_mix = jnp.concatenate([q, k, v, cq, ckv, z(HD), kr, z(LANES - HD - MLA_ROPE)], axis=1).astype(BF16)
    w_gates = gates.astype(BF16)

    per = HD + MLA_ROPE
    uq = mla_w_uq.reshape(MLA_Q_RANK, HEADS, per)
    wuq = jnp.concatenate([uq, jnp.zeros((MLA_Q_RANK, HEADS, LANES - per), uq.dtype)], axis=2)
    wuq = wuq.reshape(MLA_Q_RANK, HEADS * LANES).astype(BF16)

    uk_pad = jnp.concatenate([mla_w_uk, jnp.zeros((MLA_KV_RANK, HEADS, LANES - HD), mla_w_uk.dtype)], axis=2)
    wuk = uk_pad.reshape(MLA_KV_RANK, HEADS * LANES).astype(BF16)
    wuv = mla_w_uv.reshape(MLA_KV_RANK, WIDTH).astype(BF16)

    eye = jnp.eye(HEADS, dtype=F32)
    ukt = jnp.transpose(mla_w_uk, (1, 2, 0))
    ukt = jnp.concatenate([ukt, jnp.zeros((HEADS, LANES - HD, MLA_KV_RANK), F32)], axis=1)
    wqlat = (ukt[:, :, None, :] * eye[:, None, :, None]).reshape(HEADS * LANES, HEADS * MLA_KV_RANK).astype(BF16)
    lane_r = jnp.zeros((LANES, MLA_ROPE), F32).at[HD + jnp.arange(MLA_ROPE), jnp.arange(MLA_ROPE)].set(1.0)
    pr = (lane_r[None, :, None, :] * eye[:, None, :, None]).reshape(HEADS * LANES, HEADS * MLA_ROPE).astype(BF16)
    pk = jnp.concatenate([lane_r, jnp.zeros((LANES, LANES - MLA_ROPE), F32)], axis=1).astype(BF16)
    uvt = jnp.transpose(mla_w_uv, (1, 0, 2))
    wuvbd = (uvt[:, :, None, :] * eye[:, None, :, None]).reshape(HEADS * MLA_KV_RANK, WIDTH).astype(BF16)
    return w_mix, w_gates, wuq, wuk, wuv, wqlat, pr, pk, wuvbd


def kernel(x_prompt, x_sample, cache_moba_k, cache_moba_v, cache_mla_ckv, cache_mla_krope, cache_mem_k, cache_mem_v, page_table, mem_prompt, ffn1_norm, ffn1_w_gate, ffn1_w_up, ffn1_w_down, mix_norm, w_in, mla_q_norm, mla_w_uq, mla_kv_norm, mla_w_uk, mla_w_uv, moba_w_o, mla_w_o, w_out, xattn_norm, mem_norm, xattn_w_q, xattn_w_k, xattn_w_v, xattn_w_o, ffn2_norm, ffn2_w_gate, ffn2_w_up, ffn2_w_down, final_norm):
    bsz, seq, _ = x_prompt.shape
    n_seq = x_sample.shape[0]
    n_pages = page_table.shape[1]
    past_len = n_pages * PAGE
    assert x_sample.shape[1] == 1 and cache_moba_k.shape[0] == 1 and n_seq == LANES
    tm = min(512, seq)

    bf = lambda a: a[0].astype(BF16)
    w_mix, w_gates, wuq, wuk, wuv, wqlat, pr, pk, wuvbd = _prep_weights(w_in[0], mla_w_uq[0], mla_w_uk[0], mla_w_uv[0])
    f1 = (ffn1_norm, bf(ffn1_w_gate), bf(ffn1_w_up), bf(ffn1_w_down))
    f2 = (ffn2_norm, bf(ffn2_w_gate), bf(ffn2_w_up), bf(ffn2_w_down))
    fn = final_norm.reshape(1, D_MODEL)
    merge_w = (mix_norm, w_gates, bf(moba_w_o), bf(mla_w_o), bf(w_out), xattn_norm, bf(xattn_w_q))
    xwo = bf(xattn_w_o)

    pos_p = jnp.arange(seq, dtype=jnp.int32)
    pos_s = jnp.full((1,), past_len, jnp.int32)
    tabs_p = _rope_tables(pos_p, MOBA_THETA, MOBA_ROT, HD, 0) + _rope_tables(pos_p, MLA_THETA, MLA_ROPE, LANES, HD)
    tabs_s = tuple(jnp.broadcast_to(t, (n_seq, LANES)) for t in
                   _rope_tables(pos_s, MOBA_THETA, MOBA_ROT, HD, 0) + _rope_tables(pos_s, MLA_THETA, MLA_ROPE, LANES, HD))

    xp = x_prompt.reshape(bsz * seq, D_MODEL)
    x1 = _ffn(xp, *f1, tm=tm, name="ffn1_prompt")
    (qf, kb, kt, vt, vt4, kmean, qcat, kcat, vtm4, ckv_p, krt) = _mix_prompt(
        x1, mix_norm, w_mix, mla_q_norm, wuq, mla_kv_norm, wuk, wuv, tabs_p, bsz=bsz, seq=seq, tm=tm)
    nq = seq // MOBA_BLOCK
    oa_t = _attn_prompt(qf, kb, vt4, kmean.reshape(bsz, nq, WIDTH), bsz=bsz, seq=seq, moba=True)
    ob_t = _attn_prompt(qcat, kcat, vtm4, None, bsz=bsz, seq=seq, moba=False)
    x2, qx = _merge(x1.reshape(bsz, seq, D_MODEL), oa_t, ob_t, None, *merge_w, tm=tm, name="merge_prompt")
    mk, mv, mkb, mvb = _memkv(mem_prompt, mem_norm, bf(xattn_w_k), bf(xattn_w_v))
    ox = _xattn_prompt(qx, mkb, mvb, tm=tm)
    y_prompt = _ffn(x2.reshape(bsz * seq, D_MODEL), *f2, tm=tm, pre=(ox.reshape(bsz * seq, WIDTH), xwo), fnorm=fn,
                    name="ffn2_prompt").reshape(bsz, seq, D_MODEL)

    p_moba_k = jnp.transpose(kt.reshape(bsz, HEADS, HD, seq), (0, 3, 1, 2))[None]
    p_moba_v = jnp.transpose(vt.reshape(bsz, HEADS, HD, seq), (0, 3, 1, 2))[None]
    p_mla_ckv = ckv_p.reshape(1, bsz, seq, MLA_KV_RANK)
    p_mla_krope = jnp.transpose(krt, (0, 2, 1))[None]
    n_mem = mem_prompt.shape[1]
    p_mem_k = mk.reshape(1, bsz, n_mem, XHEADS, XHD)
    p_mem_v = mv.reshape(1, bsz, n_mem, XHEADS, XHD)

    xs = x_sample.reshape(n_seq, D_MODEL)
    xs1 = _ffn(xs, *f1, tm=n_seq, name="ffn1_sample")
    qt, ktn, vtn, ckv_s, krt_s, ql, qr, krow = _mix_sample(
        xs1, mix_norm, w_mix, mla_q_norm, wuq, mla_kv_norm, wqlat, pr, pk, tabs_s)
    pool_kt = jnp.transpose(cache_moba_k[0], (0, 2, 3, 1))
    pool_vt = jnp.transpose(cache_moba_v[0], (0, 2, 3, 1))
    pool_rt = jnp.transpose(cache_mla_krope[0], (0, 2, 1))
    p4, idx, ps_t = _moba_sample_scores(page_table, qt, ktn, pool_kt)
    idx2 = idx[:, :, :MOBA_TOPK].reshape(n_seq, HEADS * MOBA_TOPK)
    oa_s = _moba_sample_values(page_table, idx2, p4, ps_t, vtn, pool_vt)
    olat = _mla_sample(page_table, ql.reshape(n_seq, HEADS, MLA_KV_RANK), qr.reshape(n_seq, HEADS, MLA_ROPE),
                       ckv_s.reshape(n_seq, 1, MLA_KV_RANK), krow[:, :MLA_ROPE].reshape(n_seq, 1, MLA_ROPE),
                       cache_mla_ckv[0], pool_rt)
    xs2, qxs = _merge(xs1.reshape(1, n_seq, D_MODEL), oa_s.reshape(1, WIDTH, n_seq),
                      olat.reshape(1, n_seq, HEADS * MLA_KV_RANK), wuvbd, *merge_w, tm=n_seq, name="merge_sample")
    n_mem_s = cache_mem_k.shape[2]
    memk = cache_mem_k[0].reshape(n_seq, n_mem_s * XHEADS, XHD)
    memv = cache_mem_v[0].reshape(n_seq, n_mem_s * XHEADS, XHD)
    oxs = _xattn_sample(qxs.reshape(n_seq, XHEADS, XHD), memk, memv)
    y_sample = _ffn(xs2.reshape(n_seq, D_MODEL), *f2, tm=n_seq, pre=(oxs.reshape(n_seq, WIDTH).astype(BF16), xwo),
                    fnorm=fn, name="ffn2_sample").reshape(n_seq, 1, D_MODEL)

    s_moba_k = jnp.transpose(ktn.reshape(HEADS, HD, n_seq), (2, 0, 1)).reshape(1, n_seq, 1, HEADS, HD)
    s_moba_v = jnp.transpose(vtn.reshape(HEADS, HD, n_seq), (2, 0, 1)).reshape(1, n_seq, 1, HEADS, HD)
    s_mla_ckv = ckv_s.reshape(1, n_seq, 1, MLA_KV_RANK)
    s_mla_krope = jnp.transpose(krt_s, (1, 0)).reshape(1, n_seq, 1, MLA_ROPE)

    return (y_prompt, y_sample, p_moba_k, p_moba_v, p_mla_ckv, p_mla_krope, p_mem_k, p_mem_v,
            s_moba_k, s_moba_v, s_mla_ckv, s_mla_krope)
```

```python
import functools
import math

import jax
import jax.numpy as jnp
from jax import lax
from jax.experimental import pallas as pl
from jax.experimental.pallas import tpu as pltpu

F32 = jnp.float32
BF16 = jnp.bfloat16

D_MODEL = 1024
PAGE = 128
HEADS = 8
HD = 64
WIDTH = HEADS * HD
MOBA_BLOCK = 256
MOBA_TOPK = 3
MOBA_ROT = HD // 4
MOBA_THETA = 500000.0
MLA_ROPE = 32
MLA_THETA = 10000.0
MLA_Q_RANK = 384
MLA_KV_RANK = 256
XHEADS = 4
XHD = 128
D_FF = 2816
RMS_EPS = 1e-6
LANES = 128
NEG = -1e30
VMEM_LIMIT = 56 * 1024 * 1024

MOBA_SCALE = 1.0 / math.sqrt(HD)
MLA_SCALE = 1.0 / math.sqrt(HD + MLA_ROPE)
X_SCALE = 1.0 / math.sqrt(XHD)
LOG2E = math.log2(math.e)

C_Q, C_K, C_V, C_CQ, C_CKV, C_KR, C_END = 0, 512, 1024, 1536, 1920, 2176, 2304


def _cparams(n_axes=1, vmem=VMEM_LIMIT):
    return pltpu.CompilerParams(dimension_semantics=("arbitrary",) * n_axes, vmem_limit_bytes=vmem)


def _const_spec(shape):
    nd = len(shape)
    return pl.BlockSpec(shape, lambda *_: (0,) * nd, pipeline_mode=pl.Buffered(1))


def _dot(a, b):
    return jnp.dot(a, b, preferred_element_type=F32)


def _dot_nt(a, b):
    return lax.dot_general(a, b, (((1,), (1,)), ((), ())), preferred_element_type=F32)


def _split3(a):
    hi = a.astype(BF16)
    r1 = a - hi.astype(F32)
    mid = r1.astype(BF16)
    lo = (r1 - mid.astype(F32)).astype(BF16)
    return hi, mid, lo


def _dot_nt_f32(a, b):
    ah, am, al = _split3(a)
    bh, bm, bl = _split3(b)
    return (_dot_nt(ah, bh) + (_dot_nt(ah, bm) + _dot_nt(am, bh))
            + (_dot_nt(ah, bl) + _dot_nt(am, bm) + _dot_nt(al, bh)))


def _zero_once_ready(v):
    u = lax.bitcast_convert_type(v, jnp.uint32)
    u = lax.shift_right_logical(lax.shift_right_logical(u, jnp.uint32(16)), jnp.uint32(16))
    return lax.bitcast_convert_type(u, F32)


def _rms(x, g):
    ms = jnp.mean(x * x, axis=-1, keepdims=True)
    return x * lax.rsqrt(ms + RMS_EPS) * g


def _sigmoid(x):
    return 1.0 / (1.0 + jnp.exp(-x))


def _rope_block(x, c, a, b, half):
    up = pltpu.roll(x, LANES - half, 1)
    dn = pltpu.roll(x, half, 1)
    return x * c + up * a + dn * b


def _rope_wide(x, tabs, half):
    c, a, b = tabs
    blocks = [_rope_block(x[:, j:j + LANES], c, a, b, half) for j in range(0, x.shape[1], LANES)]
    return blocks[0] if len(blocks) == 1 else jnp.concatenate(blocks, axis=1)


V_AUG = 16
FFN_CHUNKS = 2


def _take_ffn_refs(it, pre_proj, final_norm):
    x_ref = next(it)
    pre = (next(it), next(it)) if pre_proj else None
    g_ref, wg_ref, wu_ref, wd_ref = next(it), next(it), next(it), next(it)
    fn_ref = next(it) if final_norm else None
    return x_ref, pre, g_ref, wg_ref, wu_ref, wd_ref, fn_ref


def _ffn_tile(x_ref, pre, g_ref, wg_ref, wu_ref, wd_ref, fn_ref, stages=None):
    x = x_ref[...]
    if pre is not None:
        o_ref, wo_ref = pre
        x = x + _dot(o_ref[...], wo_ref[...])
    h = _rms(x, g_ref[...]).astype(BF16)
    fc = D_FF // FFN_CHUNKS
    acc = jnp.zeros(x.shape, F32)
    for c in range(FFN_CHUNKS):
        gt = _dot(h, wg_ref[:, c * fc:(c + 1) * fc])
        up = _dot(h, wu_ref[:, c * fc:(c + 1) * fc])
        act = (gt * _sigmoid(gt) * up).astype(BF16)
        down = _dot(act, wd_ref[c * fc:(c + 1) * fc, :])
        acc = acc + down
        if stages is not None:
            stages += [gt[0:8, 0:LANES], up[0:8, 0:LANES], down[0:8, 0:LANES]]
    y = x + 0.5 * acc
    if fn_ref is not None:
        y = _rms(y, fn_ref[...])
    return y


def _ffn_kernel(*refs, pre_proj, final_norm):
    it = iter(refs)
    ffn_refs = _take_ffn_refs(it, pre_proj, final_norm)
    out_ref = next(it)
    out_ref[...] = _ffn_tile(*ffn_refs)


def _ffn_operands(x, g, wg, wu, wd, tm, pre, fnorm, row):
    args, specs = [x], [pl.BlockSpec((tm, D_MODEL), row)]
    if pre is not None:
        o, wo = pre
        args += [o, wo]
        specs += [pl.BlockSpec((tm, o.shape[1]), row), _const_spec(wo.shape)]
    args += [g, wg, wu, wd]
    specs += [_const_spec(g.shape), _const_spec(wg.shape), _const_spec(wu.shape), _const_spec(wd.shape)]
    if fnorm is not None:
        args.append(fnorm)
        specs.append(_const_spec(fnorm.shape))
    return args, specs


def _ffn(x, g, wg, wu, wd, *, tm, pre=None, fnorm=None, name):
    t = x.shape[0]
    args, specs = _ffn_operands(x, g, wg, wu, wd, tm, pre, fnorm, lambda i: (i, 0))
    kern = functools.partial(_ffn_kernel, pre_proj=pre is not None, final_norm=fnorm is not None)
    return pl.pallas_call(
        kern, grid=(t // tm,), in_specs=specs,
        out_specs=pl.BlockSpec((tm, D_MODEL), lambda i: (i, 0)),
        out_shape=jax.ShapeDtypeStruct((t, D_MODEL), F32),
        compiler_params=_cparams(1), name=name)(*args)


def _mix_prompt_kernel(x_ref, g_ref, w_ref, qn_ref, wuq_ref, kvn_ref, wuk_ref, wuv_ref,
                       mc_ref, ma_ref, mb_ref, lc_ref, la_ref, lb_ref,
                       qf_ref, kb_ref, kt_ref, vt_ref, vt4_ref, km_ref,
                       qcat_ref, kcat_ref, vtm4_ref, ckv_ref, krt_ref, *, tm):
    h = _rms(x_ref[...], g_ref[...]).astype(BF16)
    proj = _dot(h, w_ref[...])
    mtab = (mc_ref[...], ma_ref[...], mb_ref[...])
    ltab = (lc_ref[...], la_ref[...], lb_ref[...])
    nblk = tm // MOBA_BLOCK

    q = _rope_wide(proj[:, C_Q:C_K], mtab, MOBA_ROT // 2)
    k = _rope_wide(proj[:, C_K:C_V], mtab, MOBA_ROT // 2)
    v = proj[:, C_V:C_CQ]
    qf_ref[...] = q
    kb_ref[...] = k.astype(BF16)
    kt_ref[0] = k.T
    vt = v.T
    vt_ref[0] = vt
    for j in range(nblk):
        vt4_ref[0, j] = vt[:, j * MOBA_BLOCK:(j + 1) * MOBA_BLOCK].astype(BF16)
        km_ref[0, j:j + 1, :] = jnp.sum(k[j * MOBA_BLOCK:(j + 1) * MOBA_BLOCK], axis=0, keepdims=True) * (1.0 / MOBA_BLOCK)

    cqn = _rms(proj[:, C_CQ:C_CKV], qn_ref[...]).astype(BF16)
    qh = _rope_wide(_dot(cqn, wuq_ref[...]), ltab, MLA_ROPE // 2)
    qcat_ref[...] = (qh * (MLA_SCALE * LOG2E)).astype(BF16)

    ckvn = _rms(proj[:, C_CKV:C_KR], kvn_ref[...])
    ckv_ref[...] = ckvn
    kr = _rope_block(proj[:, C_KR:C_END], *ltab, MLA_ROPE // 2)
    krt_ref[0] = kr.T[HD:HD + MLA_ROPE, :]
    cb = ckvn.astype(BF16)
    kn = _dot(cb, wuk_ref[...])
    kcat_ref[...] = (kn + jnp.concatenate([kr] * HEADS, axis=1)).astype(BF16)
    vtm = _dot(cb, wuv_ref[...]).T
    for j in range(nblk):
        vtm4_ref[0, j] = vtm[:, j * MOBA_BLOCK:(j + 1) * MOBA_BLOCK].astype(BF16)


def _mix_prompt(x, g, w, qn, wuq, kvn, wuk, wuv, tabs, *, bsz, seq, tm):
    t = bsz * seq
    spt = seq // tm
    nblk = tm // MOBA_BLOCK
    nb = seq // MOBA_BLOCK
    row = lambda i: (i, 0)
    tab_spec = pl.BlockSpec((tm, LANES), lambda i: (i % spt, 0))
    in_specs = [pl.BlockSpec((tm, D_MODEL), row), _const_spec(g.shape), _const_spec(w.shape),
                _const_spec(qn.shape), _const_spec(wuq.shape), _const_spec(kvn.shape),
                _const_spec(wuk.shape), _const_spec(wuv.shape)] + [tab_spec] * 6
    tcol = lambda i: (i // spt, 0, i % spt)
    t4 = lambda i: (i // spt, i % spt, 0, 0)
    out_shape = [
        jax.ShapeDtypeStruct((t, WIDTH), F32),
        jax.ShapeDtypeStruct((t, WIDTH), BF16),
        jax.ShapeDtypeStruct((bsz, WIDTH, seq), F32),
        jax.ShapeDtypeStruct((bsz, WIDTH, seq), F32),
        jax.ShapeDtypeStruct((bsz, nb, WIDTH, MOBA_BLOCK), BF16),
        jax.ShapeDtypeStruct((t // tm, nblk, WIDTH), F32),
        jax.ShapeDtypeStruct((t, HEADS * LANES), BF16),
        jax.ShapeDtypeStruct((t, HEADS * LANES), BF16),
        jax.ShapeDtypeStruct((bsz, nb, WIDTH, MOBA_BLOCK), BF16),
        jax.ShapeDtypeStruct((t, MLA_KV_RANK), F32),
        jax.ShapeDtypeStruct((bsz, MLA_ROPE, seq), F32),
    ]
    out_specs = [
        pl.BlockSpec((tm, WIDTH), row), pl.BlockSpec((tm, WIDTH), row),
        pl.BlockSpec((1, WIDTH, tm), tcol), pl.BlockSpec((1, WIDTH, tm), tcol),
        pl.BlockSpec((1, nblk, WIDTH, MOBA_BLOCK), t4),
        pl.BlockSpec((1, nblk, WIDTH), lambda i: (i, 0, 0)),
        pl.BlockSpec((tm, HEADS * LANES), row), pl.BlockSpec((tm, HEADS * LANES), row),
        pl.BlockSpec((1, nblk, WIDTH, MOBA_BLOCK), t4),
        pl.BlockSpec((tm, MLA_KV_RANK), row),
        pl.BlockSpec((1, MLA_ROPE, tm), tcol),
    ]
    return pl.pallas_call(
        functools.partial(_mix_prompt_kernel, tm=tm), grid=(t // tm,), in_specs=in_specs,
        out_specs=out_specs, out_shape=out_shape, compiler_params=_cparams(1), name="mix_prompt",
    )(x, g, w, qn, wuq, kvn, wuk, wuv, *tabs)


def _mix_sample_kernel(x_ref, g_ref, w_ref, qn_ref, wuq_ref, kvn_ref, wqlat_ref, pr_ref, pk_ref,
                       mc_ref, ma_ref, mb_ref, lc_ref, la_ref, lb_ref,
                       qt_ref, kt_ref, vt_ref, ckv_ref, krt_ref, ql_ref, qr_ref, krow_ref):
    h = _rms(x_ref[...], g_ref[...]).astype(BF16)
    proj = _dot(h, w_ref[...])
    mtab = (mc_ref[...], ma_ref[...], mb_ref[...])
    ltab = (lc_ref[...], la_ref[...], lb_ref[...])
    qt_ref[...] = _rope_wide(proj[:, C_Q:C_K], mtab, MOBA_ROT // 2).T
    kt_ref[...] = _rope_wide(proj[:, C_K:C_V], mtab, MOBA_ROT // 2).T
    vt_ref[...] = proj[:, C_V:C_CQ].T

    cqn = _rms(proj[:, C_CQ:C_CKV], qn_ref[...]).astype(BF16)
    qh = (_rope_wide(_dot(cqn, wuq_ref[...]), ltab, MLA_ROPE // 2) * MLA_SCALE).astype(BF16)
    ql_ref[...] = _dot(qh, wqlat_ref[...])
    qr_ref[...] = _dot(qh, pr_ref[...])

    ckv_ref[...] = _rms(proj[:, C_CKV:C_KR], kvn_ref[...])
    kr = _rope_block(proj[:, C_KR:C_END], *ltab, MLA_ROPE // 2)
    krt_ref[...] = kr.T[HD:HD + MLA_ROPE, :]
    krow_ref[...] = _dot(kr.astype(BF16), pk_ref[...])


def _mix_sample(x, g, w, qn, wuq, kvn, wqlat, pr, pk, tabs):
    n = x.shape[0]
    args = (x, g, w, qn, wuq, kvn, wqlat, pr, pk) + tuple(tabs)
    out_shape = [
        jax.ShapeDtypeStruct((WIDTH, n), F32), jax.ShapeDtypeStruct((WIDTH, n), F32),
        jax.ShapeDtypeStruct((WIDTH, n), F32), jax.ShapeDtypeStruct((n, MLA_KV_RANK), F32),
        jax.ShapeDtypeStruct((MLA_ROPE, n), F32), jax.ShapeDtypeStruct((n, HEADS * MLA_KV_RANK), F32),
        jax.ShapeDtypeStruct((n, HEADS * MLA_ROPE), F32), jax.ShapeDtypeStruct((n, LANES), F32),
    ]
    return pl.pallas_call(
        _mix_sample_kernel, grid=(1,), in_specs=[_const_spec(a.shape) for a in args],
        out_specs=[_const_spec(s.shape) for s in out_shape], out_shape=out_shape,
        compiler_params=_cparams(1), name="mix_sample")(*args)


def _attn_prompt_kernel(*refs, moba):
    if moba:
        q_ref, k_ref, vt_ref, km_ref, o_ref, m_ref, acc_ref, qs_ref, bias_ref = refs
    else:
        q_ref, k_ref, vt_ref, o_ref, m_ref, acc_ref = refs
    qi = pl.program_id(1)
    blk = MOBA_BLOCK

    if moba:
        nb = km_ref.shape[1]
        lane = lax.broadcasted_iota(jnp.int32, (1, LANES), 1)
        bidx = lax.broadcasted_iota(jnp.int32, (nb, 1), 0)
        past = bidx < qi
        wins_tie = [jnp.where(m < bidx, 1, 0) for m in range(nb)]
        for h in range(HEADS):
            pair = slice((h // 2) * LANES, (h // 2 + 1) * LANES)
            lo = (h % 2) * HD
            qh = jnp.where((lane >= lo) & (lane < lo + HD), q_ref[:, pair], 0.0)
            gate = jnp.where(past, _dot_nt_f32(km_ref[0, :, pair], qh), -jnp.inf)
            cnt = jnp.zeros(gate.shape, jnp.int32)
            for m in range(nb):
                gm = gate[m:m + 1, :]
                cnt = cnt + jnp.where(gm > gate, 1, jnp.where(gm == gate, wins_tie[m], 0))
            bias_ref[h] = jnp.where(past, jnp.where(cnt < MOBA_TOPK, 0.0, NEG), NEG)
            qs_ref[h] = (qh * (MOBA_SCALE * LOG2E)).astype(BF16)

    def q_of(h):
        return qs_ref[h] if moba else q_ref[:, h * LANES:(h + 1) * LANES]

    def k_of(h, j):
        col = h // 2 if moba else h
        return k_ref[pl.ds(pl.multiple_of(j * blk, blk), blk), col * LANES:(col + 1) * LANES]

    ones_rows = jnp.ones((V_AUG, blk), BF16)

    def v_of(h, j):
        return jnp.concatenate([vt_ref[0, j, h * HD:(h + 1) * HD, :], ones_rows], axis=0)

    heads = range(HEADS)
    rows = [slice(h * HD, (h + 1) * HD) for h in heads]

    kpos = lax.broadcasted_iota(jnp.int32, (blk, blk), 0)
    qpos = lax.broadcasted_iota(jnp.int32, (blk, blk), 1)
    causal = kpos <= qpos
    s0 = [_dot_nt(k_of(h, qi), q_of(h)) for h in heads]
    p0 = []
    for h in heads:
        s = jnp.where(causal, s0[h], NEG)
        m0 = jnp.max(s, axis=0, keepdims=True)
        p = jnp.exp2(s - m0)
        m_ref[h, 0:1, :] = m0
        p0.append(p.astype(BF16))
    for h in heads:
        acc_ref[h] = _dot(v_of(h, qi), p0[h])

    def body(j, carry):
        sj = [_dot_nt(k_of(h, j), q_of(h)) for h in heads]
        pj, alpha = [], []
        for h in heads:
            s = sj[h] + bias_ref[h, pl.ds(j, 1), :] if moba else sj[h]
            m = m_ref[h, 0:1, :]
            mn = jnp.maximum(m, jnp.max(s, axis=0, keepdims=True))
            a = jnp.exp2(m - mn)
            p = jnp.exp2(s - mn)
            m_ref[h, 0:1, :] = mn
            pj.append(p.astype(BF16))
            alpha.append(a)
        for h in heads:
            acc_ref[h] = acc_ref[h] * alpha[h] + _dot(v_of(h, j), pj[h])
        return carry

    def pair_body(t, carry):
        body(2 * t, carry)
        return body(2 * t + 1, carry)

    lax.fori_loop(0, lax.shift_right_logical(qi, 1), pair_body, 0)

    @pl.when(lax.rem(qi, 2) == 1)
    def _():
        body(qi - 1, 0)

    for h in heads:
        a = acc_ref[h]
        o_ref[0, rows[h], :] = a[0:HD] / a[HD:HD + 1]


def _attn_prompt(q, k, vt4, kmean, *, bsz, seq, moba):
    nq = seq // MOBA_BLOCK
    qw = q.shape[1]
    args = [q, k, vt4]
    specs = [pl.BlockSpec((MOBA_BLOCK, qw), lambda b, i: (b * nq + i, 0)),
             pl.BlockSpec((seq, qw), lambda b, i: (b, 0)),
             pl.BlockSpec((1, nq, WIDTH, MOBA_BLOCK), lambda b, i: (b, 0, 0, 0))]
    scratch = [pltpu.VMEM((HEADS, 8, MOBA_BLOCK), F32), pltpu.VMEM((HEADS, HD + V_AUG, MOBA_BLOCK), F32)]
    if moba:
        args.append(kmean)
        specs.append(pl.BlockSpec((1, nq, WIDTH), lambda b, i: (b, 0, 0)))
        scratch += [pltpu.VMEM((HEADS, MOBA_BLOCK, LANES), BF16), pltpu.VMEM((HEADS, nq, MOBA_BLOCK), F32)]
    return pl.pallas_call(
        functools.partial(_attn_prompt_kernel, moba=moba), grid=(bsz, nq),
        in_specs=specs, out_specs=pl.BlockSpec((1, WIDTH, MOBA_BLOCK), lambda b, i: (b, 0, i)),
        out_shape=jax.ShapeDtypeStruct((bsz, WIDTH, seq), F32), scratch_shapes=scratch,
        compiler_params=_cparams(2), name="attn_moba" if moba else "attn_mla")(*args)


def _merge_kernel(*refs, b_latent):
    it = iter(refs)
    x_ref, oa_ref, ob_ref = next(it), next(it), next(it)
    wuvbd_ref = next(it) if b_latent else None
    g_ref, wg_ref, woa_ref, wob_ref, wout_ref, gx_ref, wq_ref, x2_ref, qx_ref = it

    x = x_ref[0]
    h = _rms(x, g_ref[...]).astype(BF16)
    gates = _dot(h, wg_ref[...])
    oa = _dot(oa_ref[0].T.astype(BF16), woa_ref[...])
    if b_latent:
        ob_in = _dot(ob_ref[0].astype(BF16), wuvbd_ref[...]).astype(BF16)
    else:
        ob_in = ob_ref[0].T.astype(BF16)
    ob = _dot(ob_in, wob_ref[...])
    mix = _sigmoid(gates[:, :D_MODEL]) * oa + _sigmoid(gates[:, D_MODEL:]) * ob
    x2 = x + _dot(mix.astype(BF16), wout_ref[...])
    x2_ref[0] = x2
    hx = _rms(x2, gx_ref[...]).astype(BF16)
    qx_ref[0] = (_dot(hx, wq_ref[...]) * X_SCALE).astype(BF16)


def _merge(x3, oa_t, ob, wuvbd, g, wg, woa, wob, wout, gx, wq, *, tm, name):
    bsz, seq, _ = x3.shape
    b_latent = wuvbd is not None
    tok = lambda b, i: (b, i, 0)
    col = lambda b, i: (b, 0, i)
    args = [x3, oa_t, ob]
    specs = [pl.BlockSpec((1, tm, D_MODEL), tok), pl.BlockSpec((1, WIDTH, tm), col)]
    if b_latent:
        specs.append(pl.BlockSpec((1, tm, ob.shape[2]), tok))
        args.append(wuvbd)
        specs.append(_const_spec(wuvbd.shape))
    else:
        specs.append(pl.BlockSpec((1, WIDTH, tm), col))
    consts = [g, wg, woa, wob, wout, gx, wq]
    args += consts
    specs += [_const_spec(a.shape) for a in consts]
    return pl.pallas_call(
        functools.partial(_merge_kernel, b_latent=b_latent), grid=(bsz, seq // tm), in_specs=specs,
        out_specs=[pl.BlockSpec((1, tm, D_MODEL), tok), pl.BlockSpec((1, tm, WIDTH), tok)],
        out_shape=[jax.ShapeDtypeStruct((bsz, seq, D_MODEL), F32), jax.ShapeDtypeStruct((bsz, seq, WIDTH), BF16)],
        compiler_params=_cparams(2), name=name)(*args)


def _memkv_kernel(m_ref, g_ref, wk_ref, wv_ref, k_ref, v_ref, kb_ref, vb_ref):
    hm = _rms(m_ref[0], g_ref[...]).astype(BF16)
    mk = _dot(hm, wk_ref[...])
    mv = _dot(hm, wv_ref[...])
    k_ref[0] = mk
    v_ref[0] = mv
    kb_ref[0] = mk.astype(BF16)
    vb_ref[0] = mv.astype(BF16)


def _memkv(mem, g, wk, wv):
    bsz, nm, _ = mem.shape
    blk = lambda b: (b, 0, 0)
    w = XHEADS * XHD
    return pl.pallas_call(
        _memkv_kernel, grid=(bsz,),
        in_specs=[pl.BlockSpec((1, nm, D_MODEL), blk), _const_spec(g.shape), _const_spec(wk.shape), _const_spec(wv.shape)],
        out_specs=[pl.BlockSpec((1, nm, w), blk)] * 4,
        out_shape=[jax.ShapeDtypeStruct((bsz, nm, w), F32)] * 2 + [jax.ShapeDtypeStruct((bsz, nm, w), BF16)] * 2,
        compiler_params=_cparams(1), name="mem_kv")(mem, g, wk, wv)


def _xattn_prompt_kernel(q_ref, k_ref, v_ref, o_ref):
    outs = []
    for h in range(XHEADS):
        sl = slice(h * XHD, (h + 1) * XHD)
        s = _dot_nt(q_ref[0, :, sl], k_ref[0, :, sl])
        m = jnp.max(s, axis=-1, keepdims=True)
        p = jnp.exp(s - m)
        l = jnp.sum(p, axis=-1, keepdims=True)
        outs.append(_dot(p.astype(BF16), v_ref[0, :, sl]) / l)
    o_ref[0] = jnp.concatenate(outs, axis=1).astype(BF16)


def _xattn_prompt(qx, mk, mv, *, tm):
    bsz, seq, w = qx.shape
    nm = mk.shape[1]
    tok = lambda b, i: (b, i, 0)
    mem = lambda b, i: (b, 0, 0)
    return pl.pallas_call(
        _xattn_prompt_kernel, grid=(bsz, seq // tm),
        in_specs=[pl.BlockSpec((1, tm, w), tok), pl.BlockSpec((1, nm, w), mem), pl.BlockSpec((1, nm, w), mem)],
        out_specs=pl.BlockSpec((1, tm, w), tok), out_shape=jax.ShapeDtypeStruct((bsz, seq, w), BF16),
        compiler_params=_cparams(2), name="xattn_prompt")(qx, mk, mv)


def _xattn_sample_kernel(q_ref, k_ref, v_ref, o_ref):
    rows = k_ref.shape[1]
    col = lax.broadcasted_iota(jnp.int32, (XHEADS, rows), 1)
    hd = lax.broadcasted_iota(jnp.int32, (XHEADS, rows), 0)
    own = (col % XHEADS) == hd
    for g in range(q_ref.shape[0]):
        s = jnp.where(own, _dot_nt(q_ref[g], k_ref[g].astype(BF16)), NEG)
        m = jnp.max(s, axis=-1, keepdims=True)
        p = jnp.exp(s - m)
        l = jnp.sum(p, axis=-1, keepdims=True)
        o_ref[g] = _dot(p.astype(BF16), v_ref[g].astype(BF16)) / l


def _xattn_sample(qx, memk, memv, group=4):
    n = qx.shape[0]
    rows = memk.shape[1]
    blk = lambda b: (b, 0, 0)
    return pl.pallas_call(
        _xattn_sample_kernel, grid=(n // group,),
        in_specs=[pl.BlockSpec((group, XHEADS, XHD), blk), pl.BlockSpec((group, rows, XHD), blk),
                  pl.BlockSpec((group, rows, XHD), blk)],
        out_specs=pl.BlockSpec((group, XHEADS, XHD), blk), out_shape=jax.ShapeDtypeStruct((n, XHEADS, XHD), F32),
        compiler_params=_cparams(1), name="xattn_sample")(qx, memk, memv)


def _ffn_scores_kernel(pt_ref, *refs, pre_proj, final_norm, n_pages, seq0, n_steps):
    it = iter(refs)
    ffn_refs = _take_ffn_refs(it, pre_proj, final_norm)
    qt_ref, ktn_ref, pool_ref = next(it), next(it), next(it)
    y_ref, p_ref, idx_ref, ps_ref = next(it), next(it), next(it), next(it)
    kbuf, sem, qcol_ref, s_ref = it

    t = pl.program_id(0)
    half = t % 2
    seq = seq0 + t // 2
    hp = n_pages // 2
    nblk = n_pages // 2

    def copies(step, slot):
        sq = seq0 + step // 2
        base = (step % 2) * hp
        return [pltpu.make_async_copy(pool_ref.at[pt_ref[sq, base + pg]], kbuf.at[slot, pg], sem.at[slot])
                for pg in range(hp)]

    @pl.when(t == 0)
    def _():
        for cp in copies(0, 0):
            cp.start()
        ps_ref[...] = jnp.zeros(ps_ref.shape, F32)

    @pl.when(t + 1 < n_steps)
    def _():
        for cp in copies(t + 1, 1 - half):
            cp.start()

    lane = lax.broadcasted_iota(jnp.int32, (1, LANES), 1)
    mine = lane == seq
    qcol = jnp.sum(jnp.where(mine, qt_ref[...], 0.0), axis=1, keepdims=True)
    qcol_ref[...] = jnp.broadcast_to(qcol, qcol_ref.shape)

    for cp in copies(t, half):
        cp.wait()

    stages = []
    y_ref[...] = _ffn_tile(*ffn_refs, stages=stages)
    after = [None, None, stages[0], stages[0], stages[1], stages[2], stages[3], stages[4]]

    for h in range(HEADS):
        qc = qcol_ref[h * HD:(h + 1) * HD, :]
        if after[h] is not None:
            qc = qc + jnp.tile(_zero_once_ready(after[h]), (HD // 8, 1))
        for pg in range(hp):
            row = half * (hp // 2) + pg // 2
            cols = slice((pg % 2) * PAGE, (pg % 2 + 1) * PAGE)
            s_ref[row, h:h + 1, cols] = jnp.sum(kbuf[half, pg, h] * qc, axis=0, keepdims=True)

    @pl.when(half == 1)
    def _():
        qk = jnp.sum(jnp.where(mine, qt_ref[...] * ktn_ref[...], 0.0), axis=1, keepdims=True)
        sub1 = lax.broadcasted_iota(jnp.int32, (HEADS, 1), 0)
        s_self = jnp.zeros((HEADS, 1), F32)
        for h in range(HEADS):
            s_self = jnp.where(sub1 == h, jnp.sum(qk[h * HD:(h + 1) * HD], axis=0, keepdims=True), s_self)
        s_self = s_self * MOBA_SCALE

        s_blk = [s_ref[n] for n in range(nblk)]
        g_col = [jnp.sum(s, axis=1, keepdims=True) for s in s_blk]
        lane8 = lax.broadcasted_iota(jnp.int32, (HEADS, LANES), 1)
        gate = jnp.zeros((HEADS, LANES), F32)
        for n in range(nblk):
            gate = jnp.where(lane8 == n, g_col[n], gate)
        cnt = jnp.zeros((HEADS, LANES), jnp.int32)
        for m in range(nblk):
            tie = jnp.where((g_col[m] == gate) & (m < lane8), 1, 0)
            cnt = cnt + jnp.where(g_col[m] > gate, 1, tie)
        cnt = jnp.where(lane8 < nblk, cnt, nblk).astype(F32)
        lanef = lane8.astype(F32)
        idx_tile = jnp.zeros((HEADS, LANES), F32)
        for r in range(MOBA_TOPK):
            idx_tile = jnp.where(lane8 == r, jnp.sum(jnp.where(cnt == r, lanef, 0.0), axis=1, keepdims=True), idx_tile)
        idx_ref[0] = idx_tile.astype(jnp.int32)

        sc = []
        for n in range(nblk):
            rank_n = jnp.sum(jnp.where(lane8 == n, cnt, 0.0), axis=1, keepdims=True)
            sc.append(jnp.where(rank_n < MOBA_TOPK, s_blk[n] * MOBA_SCALE, NEG))
        mx = sc[0]
        for n in range(1, nblk):
            mx = jnp.maximum(mx, sc[n])
        mx = jnp.maximum(jnp.max(mx, axis=1, keepdims=True), s_self)
        p_blk = [jnp.exp(s - mx) for s in sc]
        p_self = jnp.exp(s_self - mx)
        tot = p_blk[0]
        for n in range(1, nblk):
            tot = tot + p_blk[n]
        l = jnp.sum(tot, axis=1, keepdims=True) + p_self
        inv = 1.0 / l
        for n in range(nblk):
            p_ref[0, n] = p_blk[n] * inv
        p_self = p_self * inv
        for h in range(HEADS):
            rows = slice(h * HD, (h + 1) * HD)
            ps_ref[rows, :] = ps_ref[rows, :] + jnp.where(mine, jnp.broadcast_to(p_self[h:h + 1, :], (HD, LANES)), 0.0)


def _ffn_scores(x, g, wg, wu, wd, page_table, qt, ktn, pool_kt, *, seq0, n_local, pre=None, fnorm=None, name):
    t = x.shape[0]
    n_seq, n_pages = page_table.shape
    n_steps = 2 * n_local
    assert t % n_steps == 0 and n_pages % 4 == 0
    tm = t // n_steps
    nblk = n_pages // 2
    row = lambda i, pt: (i, 0)
    full = lambda i, pt: (0, 0)
    args, specs = _ffn_operands(x, g, wg, wu, wd, tm, pre, fnorm, row)
    args += [qt, ktn, pool_kt]
    specs += [pl.BlockSpec(qt.shape, full), pl.BlockSpec(ktn.shape, full), pl.BlockSpec(memory_space=pl.ANY)]
    grid_spec = pltpu.PrefetchScalarGridSpec(
        num_scalar_prefetch=1, grid=(n_steps,), in_specs=specs,
        out_specs=[pl.BlockSpec((tm, D_MODEL), row),
                   pl.BlockSpec((1, nblk, HEADS, MOBA_BLOCK), lambda i, pt: (i // 2, 0, 0, 0)),
                   pl.BlockSpec((1, HEADS, LANES), lambda i, pt: (i // 2, 0, 0)),
                   pl.BlockSpec((WIDTH, n_seq), full)],
        scratch_shapes=[pltpu.VMEM((2, n_pages // 2, HEADS, HD, PAGE), F32), pltpu.SemaphoreType.DMA((2,)),
                        pltpu.VMEM((WIDTH, LANES), F32), pltpu.VMEM((nblk, HEADS, MOBA_BLOCK), F32)])
    kern = functools.partial(_ffn_scores_kernel, pre_proj=pre is not None, final_norm=fnorm is not None,
                             n_pages=n_pages, seq0=seq0, n_steps=n_steps)
    return pl.pallas_call(
        kern, grid_spec=grid_spec,
        out_shape=[jax.ShapeDtypeStruct((t, D_MODEL), F32),
                   jax.ShapeDtypeStruct((n_local, nblk, HEADS, MOBA_BLOCK), F32),
                   jax.ShapeDtypeStruct((n_local, HEADS, LANES), jnp.int32),
                   jax.ShapeDtypeStruct((WIDTH, n_seq), F32)],
        compiler_params=_cparams(1), name=name)(page_table, *args)


def _value_copies(pt_ref, idx_ref, seq, pool_ref, buf_ref, sem):
    cps = []
    for h in range(HEADS):
        for r in range(MOBA_TOPK):
            blk = idx_ref[seq, h * MOBA_TOPK + r]
            for half in range(2):
                page = pt_ref[seq, 2 * blk + half]
                cps.append(pltpu.make_async_copy(pool_ref.at[page, h], buf_ref.at[h, r, half], sem))
    return cps


def _moba_sample_value_kernel(pt_ref, idx_ref, pa_ref, pb_ref, psa_ref, psb_ref, vtn_ref, pool_ref, o_ref, vbuf, sem,
                              *, n_seq):
    b = pl.program_id(0)
    slot = b % 2
    first = b < n_seq // 2

    @pl.when(b == 0)
    def _():
        for cp in _value_copies(pt_ref, idx_ref, 0, pool_ref, vbuf.at[0], sem.at[0]):
            cp.start()
        o_ref[...] = jnp.zeros(o_ref.shape, F32)

    @pl.when(b + 1 < n_seq)
    def _():
        for cp in _value_copies(pt_ref, idx_ref, b + 1, pool_ref, vbuf.at[1 - slot], sem.at[1 - slot]):
            cp.start()

    for cp in _value_copies(pt_ref, idx_ref, b, pool_ref, vbuf.at[slot], sem.at[slot]):
        cp.wait()

    mine = lax.broadcasted_iota(jnp.int32, (1, LANES), 1) == b
    for h in range(HEADS):
        acc = jnp.zeros((HD, PAGE), F32)
        for r in range(MOBA_TOPK):
            blk = idx_ref[b, h * MOBA_TOPK + r]
            prow = jnp.where(first, pa_ref[0, pl.ds(blk, 1), pl.ds(h, 1), :], pb_ref[0, pl.ds(blk, 1), pl.ds(h, 1), :])
            prow = prow.reshape(1, MOBA_BLOCK)
            for half in range(2):
                acc = acc + prow[:, half * PAGE:(half + 1) * PAGE] * vbuf[slot, h, r, half]
        col = jnp.sum(acc, axis=1, keepdims=True)
        rows = slice(h * HD, (h + 1) * HD)
        o_ref[rows, :] = o_ref[rows, :] + jnp.where(mine, col, 0.0)

    @pl.when(b == n_seq - 1)
    def _():
        o_ref[...] = o_ref[...] + (psa_ref[...] + psb_ref[...]) * vtn_ref[...]


def _moba_sample_values(page_table, idx, p_a, p_b, ps_a, ps_b, vtn, pool_vt):
    n_seq, n_pages = page_table.shape
    nblk = n_pages // 2
    n_half = n_seq // 2
    full = lambda b, pt, ix: (0, 0)
    pblk = (1, nblk, HEADS, MOBA_BLOCK)
    grid_spec = pltpu.PrefetchScalarGridSpec(
        num_scalar_prefetch=2, grid=(n_seq,),
        in_specs=[pl.BlockSpec(pblk, lambda b, pt, ix: (jnp.minimum(b, n_half - 1), 0, 0, 0)),
                  pl.BlockSpec(pblk, lambda b, pt, ix: (jnp.maximum(b - n_half, 0), 0, 0, 0)),
                  pl.BlockSpec(ps_a.shape, full), pl.BlockSpec(ps_b.shape, full), pl.BlockSpec(vtn.shape, full),
                  pl.BlockSpec(memory_space=pl.ANY)],
        out_specs=pl.BlockSpec((WIDTH, n_seq), full),
        scratch_shapes=[pltpu.VMEM((2, HEADS, MOBA_TOPK, 2, HD, PAGE), F32), pltpu.SemaphoreType.DMA((2,))])
    return pl.pallas_call(
        functools.partial(_moba_sample_value_kernel, n_seq=n_seq), grid_spec=grid_spec,
        out_shape=jax.ShapeDtypeStruct((WIDTH, n_seq), F32),
        compiler_params=_cparams(1), name="moba_sample_values")(page_table, idx, p_a, p_b, ps_a, ps_b, vtn, pool_vt)


def _mla_sample_kernel(pt_ref, ql_ref, qr_ref, cn_ref, kn_ref, cpool_ref, rpool_ref, o_ref,
                       cbuf, rbuf, sem, s_ref, *, n_pages, n_seq, chunk_pages):
    b = pl.program_id(0)
    slot = b % 2
    n_chunks = n_pages // chunk_pages
    ct = chunk_pages * PAGE

    def copies(seq, sl):
        cps = []
        for pg in range(n_pages):
            page = pt_ref[seq, pg]
            cps.append(pltpu.make_async_copy(cpool_ref.at[page], cbuf.at[sl, pl.ds(pg * PAGE, PAGE)], sem.at[sl]))
            cps.append(pltpu.make_async_copy(rpool_ref.at[page], rbuf.at[sl, pg], sem.at[sl]))
        return cps

    @pl.when(b == 0)
    def _():
        for cp in copies(0, 0):
            cp.start()

    @pl.when(b + 1 < n_seq)
    def _():
        for cp in copies(b + 1, 1 - slot):
            cp.start()

    ql = ql_ref[0]
    qr = qr_ref[0]
    qlb = ql.astype(BF16)
    qrb = qr.astype(BF16)
    s_self = (jnp.sum(ql * cn_ref[0], axis=1, keepdims=True) + jnp.sum(qr * kn_ref[0], axis=1, keepdims=True))

    for cp in copies(b, slot):
        cp.wait()

    def score_body(c, carry):
        start = pl.multiple_of(c * ct, ct)
        cb = cbuf[slot, pl.ds(start, ct), :].astype(BF16)
        parts = [_dot(qrb, rbuf[slot, c * chunk_pages + j].astype(BF16)) for j in range(chunk_pages)]
        s_ref[c] = _dot_nt(qlb, cb) + jnp.concatenate(parts, axis=1)
        return carry

    lax.fori_loop(0, n_chunks, score_body, 0, unroll=True)

    s3 = s_ref[...]
    mx = jnp.maximum(jnp.max(jnp.max(s3, axis=0), axis=1, keepdims=True), s_self)
    p3 = jnp.exp(s3 - mx)
    p_self = jnp.exp(s_self - mx)
    l = jnp.sum(jnp.sum(p3, axis=0), axis=1, keepdims=True) + p_self
    s_ref[...] = p3

    def pv_body(c, acc):
        start = pl.multiple_of(c * ct, ct)
        cb = cbuf[slot, pl.ds(start, ct), :].astype(BF16)
        return acc + _dot(s_ref[c].astype(BF16), cb)

    acc = lax.fori_loop(0, n_chunks, pv_body, jnp.zeros((HEADS, MLA_KV_RANK), F32), unroll=True)
    o_ref[0] = (acc + p_self * cn_ref[0]) / l


def _mla_sample(page_table, ql, qr, cn, kn, pool_c, pool_rt, chunk_pages=4):
    n_seq, n_pages = page_table.shape
    blk = lambda b, pt: (b, 0, 0)
    n_chunks = n_pages // chunk_pages
    grid_spec = pltpu.PrefetchScalarGridSpec(
        num_scalar_prefetch=1, grid=(n_seq,),
        in_specs=[pl.BlockSpec((1, HEADS, MLA_KV_RANK), blk), pl.BlockSpec((1, HEADS, MLA_ROPE), blk),
                  pl.BlockSpec((1, 1, MLA_KV_RANK), blk), pl.BlockSpec((1, 1, MLA_ROPE), blk),
                  pl.BlockSpec(memory_space=pl.ANY), pl.BlockSpec(memory_space=pl.ANY)],
        out_specs=pl.BlockSpec((1, HEADS, MLA_KV_RANK), blk),
        scratch_shapes=[pltpu.VMEM((2, n_pages * PAGE, MLA_KV_RANK), F32),
                        pltpu.VMEM((2, n_pages, MLA_ROPE, PAGE), F32),
                        pltpu.SemaphoreType.DMA((2,)),
                        pltpu.VMEM((n_chunks, HEADS, chunk_pages * PAGE), F32)])
    return pl.pallas_call(
        functools.partial(_mla_sample_kernel, n_pages=n_pages, n_seq=n_seq, chunk_pages=chunk_pages),
        grid_spec=grid_spec, out_shape=jax.ShapeDtypeStruct((n_seq, HEADS, MLA_KV_RANK), F32),
        compiler_params=_cparams(1), name="mla_sample")(page_table, ql, qr, cn, kn, pool_c, pool_rt)


def _rope_tables(pos, theta, n_rot, period, offset):
    half = n_rot // 2
    inv = theta ** (-jnp.arange(half, dtype=F32) * 2.0 / n_rot)
    ang = pos.astype(F32)[:, None] * inv[None, :]
    cos, sin = jnp.cos(ang), jnp.sin(ang)
    n = pos.shape[0]
    one, zero = jnp.ones((n, 1), F32), jnp.zeros((n, 1), F32)
    unit_c = jnp.concatenate([jnp.broadcast_to(one, (n, offset)), cos, cos,
                              jnp.broadcast_to(one, (n, period - offset - n_rot))], axis=1)
    unit_a = jnp.concatenate([jnp.broadcast_to(zero, (n, offset)), -sin, jnp.broadcast_to(zero, (n, half)),
                              jnp.broadcast_to(zero, (n, period - offset - n_rot))], axis=1)
    unit_b = jnp.concatenate([jnp.broadcast_to(zero, (n, offset)), jnp.broadcast_to(zero, (n, half)), sin,
                              jnp.broadcast_to(zero, (n, period - offset - n_rot))], axis=1)
    rep = LANES // period
    return tuple(jnp.tile(u, (1, rep)) for u in (unit_c, unit_a, unit_b))


def _prep_weights(w_in, mla_w_uq, mla_w_uk, mla_w_uv):
    w = w_in
    q, k, v = w[:, 0:512], w[:, 512:1024], w[:, 1024:1536]
    cq, ckv = w[:, 1536:1920], w[:, 1920:2176]
    kr, gates = w[:, 2176:2208], w[:, 2208:]
    z = lambda n: jnp.zeros((D_MODEL, n), w.dtype)
    w_mix = jnp.concatenate([q, k, v, cq, ckv, z(HD), kr, z(LANES - HD - MLA_ROPE)], axis=1).astype(BF16)
    w_gates = gates.astype(BF16)

    per = HD + MLA_ROPE
    uq = mla_w_uq.reshape(MLA_Q_RANK, HEADS, per)
    wuq = jnp.concatenate([uq, jnp.zeros((MLA_Q_RANK, HEADS, LANES - per), uq.dtype)], axis=2)
    wuq = wuq.reshape(MLA_Q_RANK, HEADS * LANES).astype(BF16)

    uk_pad = jnp.concatenate([mla_w_uk, jnp.zeros((MLA_KV_RANK, HEADS, LANES - HD), mla_w_uk.dtype)], axis=2)
    wuk = uk_pad.reshape(MLA_KV_RANK, HEADS * LANES).astype(BF16)
    wuv = mla_w_uv.reshape(MLA_KV_RANK, WIDTH).astype(BF16)

    eye = jnp.eye(HEADS, dtype=F32)
    ukt = jnp.transpose(mla_w_uk, (1, 2, 0))
    ukt = jnp.concatenate([ukt, jnp.zeros((HEADS, LANES - HD, MLA_KV_RANK), F32)], axis=1)
    wqlat = (ukt[:, :, None, :] * eye[:, None, :, None]).reshape(HEADS * LANES, HEADS * MLA_KV_RANK).astype(BF16)
    lane_r = jnp.zeros((LANES, MLA_ROPE), F32).at[HD + jnp.arange(MLA_ROPE), jnp.arange(MLA_ROPE)].set(1.0)
    pr = (lane_r[None, :, None, :] * eye[:, None, :, None]).reshape(HEADS * LANES, HEADS * MLA_ROPE).astype(BF16)
    pk = jnp.concatenate([lane_r, jnp.zeros((LANES, LANES - MLA_ROPE), F32)], axis=1).astype(BF16)
    uvt = jnp.transpose(mla_w_uv, (1, 0, 2))
    wuvbd = (uvt[:, :, None, :] * eye[:, None, :, None]).reshape(HEADS * MLA_KV_RANK, WIDTH).astype(BF16)
    return w_mix, w_gates, wuq, wuk, wuv, wqlat, pr, pk, wuvbd


def kernel(x_prompt, x_sample, cache_moba_k, cache_moba_v, cache_mla_ckv, cache_mla_krope, cache_mem_k, cache_mem_v, page_table, mem_prompt, ffn1_norm, ffn1_w_gate, ffn1_w_up, ffn1_w_down, mix_norm, w_in, mla_q_norm, mla_w_uq, mla_kv_norm, mla_w_uk, mla_w_uv, moba_w_o, mla_w_o, w_out, xattn_norm, mem_norm, xattn_w_q, xattn_w_k, xattn_w_v, xattn_w_o, ffn2_norm, ffn2_w_gate, ffn2_w_up, ffn2_w_down, final_norm):
    bsz, seq, _ = x_prompt.shape
    n_seq = x_sample.shape[0]
    n_pages = page_table.shape[1]
    past_len = n_pages * PAGE
    assert x_sample.shape[1] == 1 and cache_moba_k.shape[0] == 1 and n_seq == LANES
    tm = min(512, seq)

    bf = lambda a: a[0].astype(BF16)
    w_mix, w_gates, wuq, wuk, wuv, wqlat, pr, pk, wuvbd = _prep_weights(w_in[0], mla_w_uq[0], mla_w_uk[0], mla_w_uv[0])
    f1 = (ffn1_norm, bf(ffn1_w_gate), bf(ffn1_w_up), bf(ffn1_w_down))
    f2 = (ffn2_norm, bf(ffn2_w_gate), bf(ffn2_w_up), bf(ffn2_w_down))
    fn = final_norm.reshape(1, D_MODEL)
    merge_w = (mix_norm, w_gates, bf(moba_w_o), bf(mla_w_o), bf(w_out), xattn_norm, bf(xattn_w_q))
    xwo = bf(xattn_w_o)

    pos_p = jnp.arange(seq, dtype=jnp.int32)
    pos_s = jnp.full((1,), past_len, jnp.int32)
    tabs_p = _rope_tables(pos_p, MOBA_THETA, MOBA_ROT, HD, 0) + _rope_tables(pos_p, MLA_THETA, MLA_ROPE, LANES, HD)
    tabs_s = tuple(jnp.broadcast_to(t, (n_seq, LANES)) for t in
                   _rope_tables(pos_s, MOBA_THETA, MOBA_ROT, HD, 0) + _rope_tables(pos_s, MLA_THETA, MLA_ROPE, LANES, HD))

    xs = x_sample.reshape(n_seq, D_MODEL)
    xs1 = _ffn(xs, *f1, tm=n_seq, name="ffn1_sample")
    qt, ktn, vtn, ckv_s, krt_s, ql, qr, krow = _mix_sample(
        xs1, mix_norm, w_mix, mla_q_norm, wuq, mla_kv_norm, wqlat, pr, pk, tabs_s)
    pool_kt = jnp.transpose(cache_moba_k[0], (0, 2, 3, 1))
    pool_vt = jnp.transpose(cache_moba_v[0], (0, 2, 3, 1))
    pool_rt = jnp.transpose(cache_mla_krope[0], (0, 2, 1))
    n_half = n_seq // 2

    xp = x_prompt.reshape(bsz * seq, D_MODEL)
    x1, p_a, idx_a, ps_a = _ffn_scores(xp, *f1, page_table, qt, ktn, pool_kt, seq0=0, n_local=n_half,
                                       name="ffn1_prompt_scores")
    (qf, kb, kt, vt, vt4, kmean, qcat, kcat, vtm4, ckv_p, krt) = _mix_prompt(
        x1, mix_norm, w_mix, mla_q_norm, wuq, mla_kv_norm, wuk, wuv, tabs_p, bsz=bsz, seq=seq, tm=tm)
    nq = seq // MOBA_BLOCK
    oa_t = _attn_prompt(qf, kb, vt4, kmean.reshape(bsz, nq, WIDTH), bsz=bsz, seq=seq, moba=True)
    ob_t = _attn_prompt(qcat, kcat, vtm4, None, bsz=bsz, seq=seq, moba=False)
    x2, qx = _merge(x1.reshape(bsz, seq, D_MODEL), oa_t, ob_t, None, *merge_w, tm=tm, name="merge_prompt")
    mk, mv, mkb, mvb = _memkv(mem_prompt, mem_norm, bf(xattn_w_k), bf(xattn_w_v))
    ox = _xattn_prompt(qx, mkb, mvb, tm=tm)
    y_prompt, p_b, idx_b, ps_b = _ffn_scores(
        x2.reshape(bsz * seq, D_MODEL), *f2, page_table, qt, ktn, pool_kt, seq0=n_half, n_local=n_seq - n_half,
        pre=(ox.reshape(bsz * seq, WIDTH), xwo), fnorm=fn, name="ffn2_prompt_scores")
    y_prompt = y_prompt.reshape(bsz, seq, D_MODEL)

    p_moba_k = jnp.transpose(kt.reshape(bsz, HEADS, HD, seq), (0, 3, 1, 2))[None]
    p_moba_v = jnp.transpose(vt.reshape(bsz, HEADS, HD, seq), (0, 3, 1, 2))[None]
    p_mla_ckv = ckv_p.reshape(1, bsz, seq, MLA_KV_RANK)
    p_mla_krope = jnp.transpose(krt, (0, 2, 1))[None]
    n_mem = mem_prompt.shape[1]
    p_mem_k = mk.reshape(1, bsz, n_mem, XHEADS, XHD)
    p_mem_v = mv.reshape(1, bsz, n_mem, XHEADS, XHD)

    idx = jnp.concatenate([idx_a, idx_b], axis=0)
    idx2 = idx[:, :, :MOBA_TOPK].reshape(n_seq, HEADS * MOBA_TOPK)
    oa_s = _moba_sample_values(page_table, idx2, p_a, p_b, ps_a, ps_b, vtn, pool_vt)
    olat = _mla_sample(page_table, ql.reshape(n_seq, HEADS, MLA_KV_RANK), qr.reshape(n_seq, HEADS, MLA_ROPE),
                       ckv_s.reshape(n_seq, 1, MLA_KV_RANK), krow[:, :MLA_ROPE].reshape(n_seq, 1, MLA_ROPE),
                       cache_mla_ckv[0], pool_rt)
    xs2, qxs = _merge(xs1.reshape(1, n_seq, D_MODEL), oa_s.reshape(1, WIDTH, n_seq),
                      olat.reshape(1, n_seq, HEADS * MLA_KV_RANK), wuvbd, *merge_w, tm=n_seq, name="merge_sample")
    n_mem_s = cache_mem_k.shape[2]
    memk = cache_mem_k[0].reshape(n_seq, n_mem_s * XHEADS, XHD)
    memv = cache_mem_v[0].reshape(n_seq, n_mem_s * XHEADS, XHD)
    oxs = _xattn_sample(qxs.reshape(n_seq, XHEADS, XHD), memk, memv)
    y_sample = _ffn(xs2.reshape(n_seq, D_MODEL), *f2, tm=n_seq, pre=(oxs.reshape(n_seq, WIDTH).astype(BF16), xwo),
                    fnorm=fn, name="ffn2_sample").reshape(n_seq, 1, D_MODEL)

    s_moba_k = jnp.transpose(ktn.reshape(HEADS, HD, n_seq), (2, 0, 1)).reshape(1, n_seq, 1, HEADS, HD)
    s_moba_v = jnp.transpose(vtn.reshape(HEADS, HD, n_seq), (2, 0, 1)).reshape(1, n_seq, 1, HEADS, HD)
    s_mla_ckv = ckv_s.reshape(1, n_seq, 1, MLA_KV_RANK)
    s_mla_krope = jnp.transpose(krt_s, (1, 0)).reshape(1, n_seq, 1, MLA_ROPE)

    return (y_prompt, y_sample, p_moba_k, p_moba_v, p_mla_ckv, p_mla_krope, p_mem_k, p_mem_v,
            s_moba_k, s_moba_v, s_mla_ckv, s_mla_krope)
```

```python
import functools
import math

import jax
import jax.numpy as jnp
from jax import lax
from jax.experimental import pallas as pl
from jax.experimental.pallas import tpu as pltpu

F32 = jnp.float32
BF16 = jnp.bfloat16

D_MODEL = 1024
PAGE = 128
HEADS = 8
HD = 64
WIDTH = HEADS * HD
MOBA_BLOCK = 256
MOBA_TOPK = 3
MOBA_ROT = HD // 4
MOBA_THETA = 500000.0
MLA_ROPE = 32
MLA_THETA = 10000.0
MLA_Q_RANK = 384
MLA_KV_RANK = 256
XHEADS = 4
XHD = 128
D_FF = 2816
RMS_EPS = 1e-6
LANES = 128
NEG = -1e30
VMEM_LIMIT = 56 * 1024 * 1024

MOBA_SCALE = 1.0 / math.sqrt(HD)
MLA_SCALE = 1.0 / math.sqrt(HD + MLA_ROPE)
X_SCALE = 1.0 / math.sqrt(XHD)
LOG2E = math.log2(math.e)

C_Q, C_K, C_V, C_CQ, C_CKV, C_KR, C_END = 0, 512, 1024, 1536, 1920, 2176, 2304


def _cparams(n_axes=1, vmem=VMEM_LIMIT):
    return pltpu.CompilerParams(dimension_semantics=("arbitrary",) * n_axes, vmem_limit_bytes=vmem)


def _const_spec(shape):
    nd = len(shape)
    return pl.BlockSpec(shape, lambda *_: (0,) * nd, pipeline_mode=pl.Buffered(1))


def _dot(a, b):
    return jnp.dot(a, b, preferred_element_type=F32)


def _dot_nt(a, b):
    return lax.dot_general(a, b, (((1,), (1,)), ((), ())), preferred_element_type=F32)


def _split3(a):
    hi = a.astype(BF16)
    r1 = a - hi.astype(F32)
    mid = r1.astype(BF16)
    lo = (r1 - mid.astype(F32)).astype(BF16)
    return hi, mid, lo


def _dot_nt_f32(a, b):
    ah, am, al = _split3(a)
    bh, bm, bl = _split3(b)
    return (_dot_nt(ah, bh) + (_dot_nt(ah, bm) + _dot_nt(am, bh))
            + (_dot_nt(ah, bl) + _dot_nt(am, bm) + _dot_nt(al, bh)))


def _zero_once_ready(v):
    u = lax.bitcast_convert_type(v, jnp.uint32)
    u = lax.shift_right_logical(lax.shift_right_logical(u, jnp.uint32(16)), jnp.uint32(16))
    return lax.bitcast_convert_type(u, F32)


def _rms(x, g):
    ms = jnp.mean(x * x, axis=-1, keepdims=True)
    return x * lax.rsqrt(ms + RMS_EPS) * g


def _sigmoid(x):
    return 1.0 / (1.0 + jnp.exp(-x))


def _rope_block(x, c, a, b, half):
    up = pltpu.roll(x, LANES - half, 1)
    dn = pltpu.roll(x, half, 1)
    return x * c + up * a + dn * b


def _rope_wide(x, tabs, half):
    c, a, b = tabs
    blocks = [_rope_block(x[:, j:j + LANES], c, a, b, half) for j in range(0, x.shape[1], LANES)]
    return blocks[0] if len(blocks) == 1 else jnp.concatenate(blocks, axis=1)


V_AUG = 16
HEAD_GROUP = 4
FFN_CHUNKS = 2


def _take_ffn_refs(it, pre_proj, final_norm):
    x_ref = next(it)
    pre = (next(it), next(it)) if pre_proj else None
    g_ref, wg_ref, wu_ref, wd_ref = next(it), next(it), next(it), next(it)
    fn_ref = next(it) if final_norm else None
    return x_ref, pre, g_ref, wg_ref, wu_ref, wd_ref, fn_ref


def _ffn_begin(x_ref, pre, g_ref):
    x = x_ref[...]
    if pre is not None:
        o_ref, wo_ref = pre
        x = x + _dot(o_ref[...], wo_ref[...])
    return x, _rms(x, g_ref[...]).astype(BF16)


def _swiglu_act(gt, up):
    return (gt * _sigmoid(gt) * up).astype(BF16)


def _ffn_chunk(h, c, wg_ref, wu_ref, wd_ref):
    fc = D_FF // FFN_CHUNKS
    act = _swiglu_act(_dot(h, wg_ref[:, c * fc:(c + 1) * fc]), _dot(h, wu_ref[:, c * fc:(c + 1) * fc]))
    return _dot(act, wd_ref[c * fc:(c + 1) * fc, :])


def _ffn_end(x, acc, fn_ref):
    y = x + 0.5 * acc
    return y if fn_ref is None else _rms(y, fn_ref[...])


def _ffn_tile(x_ref, pre, g_ref, wg_ref, wu_ref, wd_ref, fn_ref):
    x, h = _ffn_begin(x_ref, pre, g_ref)
    acc = jnp.zeros(x.shape, F32)
    for c in range(FFN_CHUNKS):
        acc = acc + _ffn_chunk(h, c, wg_ref, wu_ref, wd_ref)
    return _ffn_end(x, acc, fn_ref)


def _ffn_kernel(*refs, pre_proj, final_norm):
    it = iter(refs)
    ffn_refs = _take_ffn_refs(it, pre_proj, final_norm)
    out_ref = next(it)
    out_ref[...] = _ffn_tile(*ffn_refs)


def _ffn_operands(x, g, wg, wu, wd, tm, pre, fnorm, row):
    args, specs = [x], [pl.BlockSpec((tm, D_MODEL), row)]
    if pre is not None:
        o, wo = pre
        args += [o, wo]
        specs += [pl.BlockSpec((tm, o.shape[1]), row), _const_spec(wo.shape)]
    args += [g, wg, wu, wd]
    specs += [_const_spec(g.shape), _const_spec(wg.shape), _const_spec(wu.shape), _const_spec(wd.shape)]
    if fnorm is not None:
        args.append(fnorm)
        specs.append(_const_spec(fnorm.shape))
    return args, specs


def _ffn(x, g, wg, wu, wd, *, tm, pre=None, fnorm=None, name):
    t = x.shape[0]
    args, specs = _ffn_operands(x, g, wg, wu, wd, tm, pre, fnorm, lambda i: (i, 0))
    kern = functools.partial(_ffn_kernel, pre_proj=pre is not None, final_norm=fnorm is not None)
    return pl.pallas_call(
        kern, grid=(t // tm,), in_specs=specs,
        out_specs=pl.BlockSpec((tm, D_MODEL), lambda i: (i, 0)),
        out_shape=jax.ShapeDtypeStruct((t, D_MODEL), F32),
        compiler_params=_cparams(1), name=name)(*args)


def _mix_prompt_kernel(x_ref, g_ref, w_ref, qn_ref, wuq_ref, kvn_ref, wuk_ref, wuv_ref,
                       mc_ref, ma_ref, mb_ref, lc_ref, la_ref, lb_ref,
                       qf_ref, kb_ref, kt_ref, vt_ref, vt4_ref, km_ref,
                       qcat_ref, kcat_ref, vtm4_ref, ckv_ref, krt_ref, *, tm):
    h = _rms(x_ref[...], g_ref[...]).astype(BF16)
    proj = _dot(h, w_ref[...])
    mtab = (mc_ref[...], ma_ref[...], mb_ref[...])
    ltab = (lc_ref[...], la_ref[...], lb_ref[...])
    nblk = tm // MOBA_BLOCK

    q = _rope_wide(proj[:, C_Q:C_K], mtab, MOBA_ROT // 2)
    k = _rope_wide(proj[:, C_K:C_V], mtab, MOBA_ROT // 2)
    v = proj[:, C_V:C_CQ]
    qf_ref[...] = q
    kb_ref[...] = k.astype(BF16)
    kt_ref[0] = k.T
    vt = v.T
    vt_ref[0] = vt
    for j in range(nblk):
        vt4_ref[0, j] = vt[:, j * MOBA_BLOCK:(j + 1) * MOBA_BLOCK].astype(BF16)
        km_ref[0, j:j + 1, :] = jnp.sum(k[j * MOBA_BLOCK:(j + 1) * MOBA_BLOCK], axis=0, keepdims=True) * (1.0 / MOBA_BLOCK)

    cqn = _rms(proj[:, C_CQ:C_CKV], qn_ref[...]).astype(BF16)
    qh = _rope_wide(_dot(cqn, wuq_ref[...]), ltab, MLA_ROPE // 2)
    qcat_ref[...] = (qh * (MLA_SCALE * LOG2E)).astype(BF16)

    ckvn = _rms(proj[:, C_CKV:C_KR], kvn_ref[...])
    ckv_ref[...] = ckvn
    kr = _rope_block(proj[:, C_KR:C_END], *ltab, MLA_ROPE // 2)
    krt_ref[0] = kr.T[HD:HD + MLA_ROPE, :]
    cb = ckvn.astype(BF16)
    kn = _dot(cb, wuk_ref[...])
    kcat_ref[...] = (kn + jnp.concatenate([kr] * HEADS, axis=1)).astype(BF16)
    vtm = _dot(cb, wuv_ref[...]).T
    for j in range(nblk):
        vtm4_ref[0, j] = vtm[:, j * MOBA_BLOCK:(j + 1) * MOBA_BLOCK].astype(BF16)


def _mix_prompt(x, g, w, qn, wuq, kvn, wuk, wuv, tabs, *, bsz, seq, tm):
    t = bsz * seq
    spt = seq // tm
    nblk = tm // MOBA_BLOCK
    nb = seq // MOBA_BLOCK
    row = lambda i: (i, 0)
    tab_spec = pl.BlockSpec((tm, LANES), lambda i: (i % spt, 0))
    in_specs = [pl.BlockSpec((tm, D_MODEL), row), _const_spec(g.shape), _const_spec(w.shape),
                _const_spec(qn.shape), _const_spec(wuq.shape), _const_spec(kvn.shape),
                _const_spec(wuk.shape), _const_spec(wuv.shape)] + [tab_spec] * 6
    tcol = lambda i: (i // spt, 0, i % spt)
    t4 = lambda i: (i // spt, i % spt, 0, 0)
    out_shape = [
        jax.ShapeDtypeStruct((t, WIDTH), F32),
        jax.ShapeDtypeStruct((t, WIDTH), BF16),
        jax.ShapeDtypeStruct((bsz, WIDTH, seq), F32),
        jax.ShapeDtypeStruct((bsz, WIDTH, seq), F32),
        jax.ShapeDtypeStruct((bsz, nb, WIDTH, MOBA_BLOCK), BF16),
        jax.ShapeDtypeStruct((t // tm, nblk, WIDTH), F32),
        jax.ShapeDtypeStruct((t, HEADS * LANES), BF16),
        jax.ShapeDtypeStruct((t, HEADS * LANES), BF16),
        jax.ShapeDtypeStruct((bsz, nb, WIDTH, MOBA_BLOCK), BF16),
        jax.ShapeDtypeStruct((t, MLA_KV_RANK), F32),
        jax.ShapeDtypeStruct((bsz, MLA_ROPE, seq), F32),
    ]
    out_specs = [
        pl.BlockSpec((tm, WIDTH), row), pl.BlockSpec((tm, WIDTH), row),
        pl.BlockSpec((1, WIDTH, tm), tcol), pl.BlockSpec((1, WIDTH, tm), tcol),
        pl.BlockSpec((1, nblk, WIDTH, MOBA_BLOCK), t4),
        pl.BlockSpec((1, nblk, WIDTH), lambda i: (i, 0, 0)),
        pl.BlockSpec((tm, HEADS * LANES), row), pl.BlockSpec((tm, HEADS * LANES), row),
        pl.BlockSpec((1, nblk, WIDTH, MOBA_BLOCK), t4),
        pl.BlockSpec((tm, MLA_KV_RANK), row),
        pl.BlockSpec((1, MLA_ROPE, tm), tcol),
    ]
    return pl.pallas_call(
        functools.partial(_mix_prompt_kernel, tm=tm), grid=(t // tm,), in_specs=in_specs,
        out_specs=out_specs, out_shape=out_shape, compiler_params=_cparams(1), name="mix_prompt",
    )(x, g, w, qn, wuq, kvn, wuk, wuv, *tabs)


def _mix_sample_kernel(x_ref, g_ref, w_ref, qn_ref, wuq_ref, kvn_ref, wqlat_ref, pr_ref, pk_ref,
                       mc_ref, ma_ref, mb_ref, lc_ref, la_ref, lb_ref,
                       qt_ref, kt_ref, vt_ref, ckv_ref, krt_ref, ql_ref, qr_ref, krow_ref):
    h = _rms(x_ref[...], g_ref[...]).astype(BF16)
    proj = _dot(h, w_ref[...])
    mtab = (mc_ref[...], ma_ref[...], mb_ref[...])
    ltab = (lc_ref[...], la_ref[...], lb_ref[...])
    qt_ref[...] = _rope_wide(proj[:, C_Q:C_K], mtab, MOBA_ROT // 2).T
    kt_ref[...] = _rope_wide(proj[:, C_K:C_V], mtab, MOBA_ROT // 2).T
    vt_ref[...] = proj[:, C_V:C_CQ].T

    cqn = _rms(proj[:, C_CQ:C_CKV], qn_ref[...]).astype(BF16)
    qh = (_rope_wide(_dot(cqn, wuq_ref[...]), ltab, MLA_ROPE // 2) * MLA_SCALE).astype(BF16)
    ql_ref[...] = _dot(qh, wqlat_ref[...])
    qr_ref[...] = _dot(qh, pr_ref[...])

    ckv_ref[...] = _rms(proj[:, C_CKV:C_KR], kvn_ref[...])
    kr = _rope_block(proj[:, C_KR:C_END], *ltab, MLA_ROPE // 2)
    krt_ref[...] = kr.T[HD:HD + MLA_ROPE, :]
    krow_ref[...] = _dot(kr.astype(BF16), pk_ref[...])


def _mix_sample(x, g, w, qn, wuq, kvn, wqlat, pr, pk, tabs):
    n = x.shape[0]
    args = (x, g, w, qn, wuq, kvn, wqlat, pr, pk) + tuple(tabs)
    out_shape = [
        jax.ShapeDtypeStruct((WIDTH, n), F32), jax.ShapeDtypeStruct((WIDTH, n), F32),
        jax.ShapeDtypeStruct((WIDTH, n), F32), jax.ShapeDtypeStruct((n, MLA_KV_RANK), F32),
        jax.ShapeDtypeStruct((MLA_ROPE, n), F32), jax.ShapeDtypeStruct((n, HEADS * MLA_KV_RANK), F32),
        jax.ShapeDtypeStruct((n, HEADS * MLA_ROPE), F32), jax.ShapeDtypeStruct((n, LANES), F32),
    ]
    return pl.pallas_call(
        _mix_sample_kernel, grid=(1,), in_specs=[_const_spec(a.shape) for a in args],
        out_specs=[_const_spec(s.shape) for s in out_shape], out_shape=out_shape,
        compiler_params=_cparams(1), name="mix_sample")(*args)


def _attn_prompt_kernel(*refs, moba):
    if moba:
        q_ref, k_ref, vt_ref, km_ref, o_ref, m_ref, acc_ref, qs_ref, bias_ref = refs
    else:
        q_ref, k_ref, vt_ref, o_ref, m_ref, acc_ref = refs
    qi = pl.program_id(1)
    blk = MOBA_BLOCK

    if moba:
        nb = km_ref.shape[1]
        lane = lax.broadcasted_iota(jnp.int32, (1, LANES), 1)
        bidx = lax.broadcasted_iota(jnp.int32, (nb, 1), 0)
        past = bidx < qi
        wins_tie = [jnp.where(m < bidx, 1, 0) for m in range(nb)]
        for h in range(HEADS):
            pair = slice((h // 2) * LANES, (h // 2 + 1) * LANES)
            lo = (h % 2) * HD
            qh = jnp.where((lane >= lo) & (lane < lo + HD), q_ref[:, pair], 0.0)
            gate = jnp.where(past, _dot_nt_f32(km_ref[0, :, pair], qh), -jnp.inf)
            cnt = jnp.zeros(gate.shape, jnp.int32)
            for m in range(nb):
                gm = gate[m:m + 1, :]
                cnt = cnt + jnp.where(gm > gate, 1, jnp.where(gm == gate, wins_tie[m], 0))
            bias_ref[h] = jnp.where(past, jnp.where(cnt < MOBA_TOPK, 0.0, NEG), NEG)
            qs_ref[h] = (qh * (MOBA_SCALE * LOG2E)).astype(BF16)

    def q_of(h):
        return qs_ref[h] if moba else q_ref[:, h * LANES:(h + 1) * LANES]

    def k_of(h, j):
        col = h // 2 if moba else h
        return k_ref[pl.ds(pl.multiple_of(j * blk, blk), blk), col * LANES:(col + 1) * LANES]

    ones_rows = jnp.ones((V_AUG, blk), BF16)

    def v_of(h, j):
        return jnp.concatenate([vt_ref[0, j, h * HD:(h + 1) * HD, :], ones_rows], axis=0)

    heads = range(HEADS)
    rows = [slice(h * HD, (h + 1) * HD) for h in heads]

    kpos = lax.broadcasted_iota(jnp.int32, (blk, blk), 0)
    qpos = lax.broadcasted_iota(jnp.int32, (blk, blk), 1)
    causal = kpos <= qpos

    def softmax_piece(h, j, s, own):
        if own:
            s = jnp.where(causal, s, NEG)
            mn = jnp.max(s, axis=0, keepdims=True)
            a = None
        else:
            s = s + bias_ref[h, pl.ds(j, 1), :] if moba else s
            m = m_ref[h, 0:1, :]
            mn = jnp.maximum(m, jnp.max(s, axis=0, keepdims=True))
            a = jnp.exp2(m - mn)
        m_ref[h, 0:1, :] = mn
        return jnp.exp2(s - mn).astype(BF16), a

    def run_blocks(js, own=False):
        units = [(j, range(g, g + HEAD_GROUP)) for j in js for g in range(0, HEADS, HEAD_GROUP)]
        scores, probs = {}, {}
        for k in range(len(units) + 2):
            if k < len(units):
                j, grp = units[k]
                scores[k] = [_dot_nt(k_of(h, j), q_of(h)) for h in grp]
            if 0 <= k - 1 < len(units):
                j, grp = units[k - 1]
                probs[k - 1] = [softmax_piece(h, j, s, own) for h, s in zip(grp, scores.pop(k - 1))]
            if 0 <= k - 2 < len(units):
                j, grp = units[k - 2]
                for h, (p, a) in zip(grp, probs.pop(k - 2)):
                    pv = _dot(v_of(h, j), p)
                    acc_ref[h] = pv if own else acc_ref[h] * a + pv

    run_blocks([qi], own=True)

    def pair_body(t, carry):
        run_blocks([2 * t, 2 * t + 1])
        return carry

    lax.fori_loop(0, lax.shift_right_logical(qi, 1), pair_body, 0)

    @pl.when(lax.rem(qi, 2) == 1)
    def _():
        run_blocks([qi - 1])

    for h in heads:
        a = acc_ref[h]
        o_ref[0, rows[h], :] = a[0:HD] / a[HD:HD + 1]


def _attn_prompt(q, k, vt4, kmean, *, bsz, seq, moba):
    nq = seq // MOBA_BLOCK
    qw = q.shape[1]
    args = [q, k, vt4]
    specs = [pl.BlockSpec((MOBA_BLOCK, qw), lambda b, i: (b * nq + i, 0)),
             pl.BlockSpec((seq, qw), lambda b, i: (b, 0)),
             pl.BlockSpec((1, nq, WIDTH, MOBA_BLOCK), lambda b, i: (b, 0, 0, 0))]
    scratch = [pltpu.VMEM((HEADS, 8, MOBA_BLOCK), F32), pltpu.VMEM((HEADS, HD + V_AUG, MOBA_BLOCK), F32)]
    if moba:
        args.append(kmean)
        specs.append(pl.BlockSpec((1, nq, WIDTH), lambda b, i: (b, 0, 0)))
        scratch += [pltpu.VMEM((HEADS, MOBA_BLOCK, LANES), BF16), pltpu.VMEM((HEADS, nq, MOBA_BLOCK), F32)]
    return pl.pallas_call(
        functools.partial(_attn_prompt_kernel, moba=moba), grid=(bsz, nq),
        in_specs=specs, out_specs=pl.BlockSpec((1, WIDTH, MOBA_BLOCK), lambda b, i: (b, 0, i)),
        out_shape=jax.ShapeDtypeStruct((bsz, WIDTH, seq), F32), scratch_shapes=scratch,
        compiler_params=_cparams(2), name="attn_moba" if moba else "attn_mla")(*args)


def _merge_kernel(*refs, b_latent):
    it = iter(refs)
    x_ref, oa_ref, ob_ref = next(it), next(it), next(it)
    wuvbd_ref = next(it) if b_latent else None
    g_ref, wg_ref, woa_ref, wob_ref, wout_ref, gx_ref, wq_ref, x2_ref, qx_ref = it

    x = x_ref[0]
    h = _rms(x, g_ref[...]).astype(BF16)
    gates = _dot(h, wg_ref[...])
    oa = _dot(oa_ref[0].T.astype(BF16), woa_ref[...])
    if b_latent:
        ob_in = _dot(ob_ref[0].astype(BF16), wuvbd_ref[...]).astype(BF16)
    else:
        ob_in = ob_ref[0].T.astype(BF16)
    ob = _dot(ob_in, wob_ref[...])
    mix = _sigmoid(gates[:, :D_MODEL]) * oa + _sigmoid(gates[:, D_MODEL:]) * ob
    x2 = x + _dot(mix.astype(BF16), wout_ref[...])
    x2_ref[0] = x2
    hx = _rms(x2, gx_ref[...]).astype(BF16)
    qx_ref[0] = (_dot(hx, wq_ref[...]) * X_SCALE).astype(BF16)


def _merge(x3, oa_t, ob, wuvbd, g, wg, woa, wob, wout, gx, wq, *, tm, name):
    bsz, seq, _ = x3.shape
    b_latent = wuvbd is not None
    tok = lambda b, i: (b, i, 0)
    col = lambda b, i: (b, 0, i)
    args = [x3, oa_t, ob]
    specs = [pl.BlockSpec((1, tm, D_MODEL), tok), pl.BlockSpec((1, WIDTH, tm), col)]
    if b_latent:
        specs.append(pl.BlockSpec((1, tm, ob.shape[2]), tok))
        args.append(wuvbd)
        specs.append(_const_spec(wuvbd.shape))
    else:
        specs.append(pl.BlockSpec((1, WIDTH, tm), col))
    consts = [g, wg, woa, wob, wout, gx, wq]
    args += consts
    specs += [_const_spec(a.shape) for a in consts]
    return pl.pallas_call(
        functools.partial(_merge_kernel, b_latent=b_latent), grid=(bsz, seq // tm), in_specs=specs,
        out_specs=[pl.BlockSpec((1, tm, D_MODEL), tok), pl.BlockSpec((1, tm, WIDTH), tok)],
        out_shape=[jax.ShapeDtypeStruct((bsz, seq, D_MODEL), F32), jax.ShapeDtypeStruct((bsz, seq, WIDTH), BF16)],
        compiler_params=_cparams(2), name=name)(*args)


def _memkv_kernel(m_ref, g_ref, wk_ref, wv_ref, k_ref, v_ref, kb_ref, vb_ref):
    hm = _rms(m_ref[0], g_ref[...]).astype(BF16)
    mk = _dot(hm, wk_ref[...])
    mv = _dot(hm, wv_ref[...])
    k_ref[0] = mk
    v_ref[0] = mv
    kb_ref[0] = mk.astype(BF16)
    vb_ref[0] = mv.astype(BF16)


def _memkv(mem, g, wk, wv):
    bsz, nm, _ = mem.shape
    blk = lambda b: (b, 0, 0)
    w = XHEADS * XHD
    return pl.pallas_call(
        _memkv_kernel, grid=(bsz,),
        in_specs=[pl.BlockSpec((1, nm, D_MODEL), blk), _const_spec(g.shape), _const_spec(wk.shape), _const_spec(wv.shape)],
        out_specs=[pl.BlockSpec((1, nm, w), blk)] * 4,
        out_shape=[jax.ShapeDtypeStruct((bsz, nm, w), F32)] * 2 + [jax.ShapeDtypeStruct((bsz, nm, w), BF16)] * 2,
        compiler_params=_cparams(1), name="mem_kv")(mem, g, wk, wv)


def _xattn_prompt_kernel(q_ref, k_ref, v_ref, o_ref):
    outs = []
    for h in range(XHEADS):
        sl = slice(h * XHD, (h + 1) * XHD)
        s = _dot_nt(q_ref[0, :, sl], k_ref[0, :, sl])
        m = jnp.max(s, axis=-1, keepdims=True)
        p = jnp.exp(s - m)
        l = jnp.sum(p, axis=-1, keepdims=True)
        outs.append(_dot(p.astype(BF16), v_ref[0, :, sl]) / l)
    o_ref[0] = jnp.concatenate(outs, axis=1).astype(BF16)


def _xattn_prompt(qx, mk, mv, *, tm):
    bsz, seq, w = qx.shape
    nm = mk.shape[1]
    tok = lambda b, i: (b, i, 0)
    mem = lambda b, i: (b, 0, 0)
    return pl.pallas_call(
        _xattn_prompt_kernel, grid=(bsz, seq // tm),
        in_specs=[pl.BlockSpec((1, tm, w), tok), pl.BlockSpec((1, nm, w), mem), pl.BlockSpec((1, nm, w), mem)],
        out_specs=pl.BlockSpec((1, tm, w), tok), out_shape=jax.ShapeDtypeStruct((bsz, seq, w), BF16),
        compiler_params=_cparams(2), name="xattn_prompt")(qx, mk, mv)


def _xattn_sample_kernel(q_ref, k_ref, v_ref, o_ref):
    rows = k_ref.shape[1]
    col = lax.broadcasted_iota(jnp.int32, (XHEADS, rows), 1)
    hd = lax.broadcasted_iota(jnp.int32, (XHEADS, rows), 0)
    own = (col % XHEADS) == hd
    for g in range(q_ref.shape[0]):
        s = jnp.where(own, _dot_nt(q_ref[g], k_ref[g].astype(BF16)), NEG)
        m = jnp.max(s, axis=-1, keepdims=True)
        p = jnp.exp(s - m)
        l = jnp.sum(p, axis=-1, keepdims=True)
        o_ref[g] = _dot(p.astype(BF16), v_ref[g].astype(BF16)) / l


def _xattn_sample(qx, memk, memv, group=4):
    n = qx.shape[0]
    rows = memk.shape[1]
    blk = lambda b: (b, 0, 0)
    return pl.pallas_call(
        _xattn_sample_kernel, grid=(n // group,),
        in_specs=[pl.BlockSpec((group, XHEADS, XHD), blk), pl.BlockSpec((group, rows, XHD), blk),
                  pl.BlockSpec((group, rows, XHD), blk)],
        out_specs=pl.BlockSpec((group, XHEADS, XHD), blk), out_shape=jax.ShapeDtypeStruct((n, XHEADS, XHD), F32),
        compiler_params=_cparams(1), name="xattn_sample")(qx, memk, memv)


def _ffn_scores_kernel(pt_ref, *refs, pre_proj, final_norm, n_pages, seq0, n_steps):
    it = iter(refs)
    ffn_refs = _take_ffn_refs(it, pre_proj, final_norm)
    qt_ref, ktn_ref, pool_ref = next(it), next(it), next(it)
    y_ref, p_ref, idx_ref, ps_ref = next(it), next(it), next(it), next(it)
    kbuf, sem, qcol_ref, s_ref = it

    t = pl.program_id(0)
    half = t % 2
    seq = seq0 + t // 2
    hp = n_pages // 2
    nblk = n_pages // 2

    def copies(step, slot):
        sq = seq0 + step // 2
        base = (step % 2) * hp
        return [pltpu.make_async_copy(pool_ref.at[pt_ref[sq, base + pg]], kbuf.at[slot, pg], sem.at[slot])
                for pg in range(hp)]

    @pl.when(t == 0)
    def _():
        for cp in copies(0, 0):
            cp.start()
        ps_ref[...] = jnp.zeros(ps_ref.shape, F32)
        s_ref[...] = jnp.zeros(s_ref.shape, F32)

    lane = lax.broadcasted_iota(jnp.int32, (1, LANES), 1)
    mine = lane == seq
    qcol = jnp.sum(jnp.where(mine, qt_ref[...], 0.0), axis=1, keepdims=True)
    qcol_ref[...] = jnp.broadcast_to(qcol, qcol_ref.shape)

    x_ref, pre, g_ref, wg_ref, wu_ref, wd_ref, fn_ref = ffn_refs
    fc = D_FF // FFN_CHUNKS
    corner = lambda a: a[0:8, 0:LANES]
    x, hn = _ffn_begin(x_ref, pre, g_ref)
    act0 = _swiglu_act(_dot(hn, wg_ref[:, 0:fc]), _dot(hn, wu_ref[:, 0:fc]))

    nxt = jnp.where(t + 1 < n_steps, t + 1, 0)
    for cp in copies(nxt, 1 - half):
        cp.start()

    for cp in copies(t, half):
        cp.wait()

    down0 = _dot(act0, wd_ref[0:fc, :])
    gt1 = _dot(hn, wg_ref[:, fc:2 * fc])
    up1 = _dot(hn, wu_ref[:, fc:2 * fc])
    down1 = _dot(_swiglu_act(gt1, up1), wd_ref[fc:2 * fc, :])
    y_ref[...] = _ffn_end(x, down0 + down1, fn_ref)
    after = [None, None, corner(down0), corner(down0), corner(gt1), corner(gt1), corner(up1), corner(up1)]

    for h in range(HEADS):
        qc = qcol_ref[h * HD:(h + 1) * HD, :]
        if after[h] is not None:
            qc = qc + jnp.tile(_zero_once_ready(after[h]), (HD // 8, 1))
        for pg in range(hp):
            row = half * (hp // 2) + pg // 2
            cols = slice((pg % 2) * PAGE, (pg % 2 + 1) * PAGE)
            s_ref[row, h:h + 1, cols] = jnp.sum(kbuf[half, pg, h] * qc, axis=0, keepdims=True)

    done = lane == jnp.where(half == 1, seq, -1)

    def finalize():
        qk = jnp.sum(jnp.where(mine, qt_ref[...] * ktn_ref[...], 0.0), axis=1, keepdims=True)
        sub1 = lax.broadcasted_iota(jnp.int32, (HEADS, 1), 0)
        s_self = jnp.zeros((HEADS, 1), F32)
        for h in range(HEADS):
            s_self = jnp.where(sub1 == h, jnp.sum(qk[h * HD:(h + 1) * HD], axis=0, keepdims=True), s_self)
        s_self = s_self * MOBA_SCALE

        s_blk = [s_ref[n] for n in range(nblk)]
        g_col = [jnp.sum(s, axis=1, keepdims=True) for s in s_blk]
        lane8 = lax.broadcasted_iota(jnp.int32, (HEADS, LANES), 1)
        gate = jnp.zeros((HEADS, LANES), F32)
        for n in range(nblk):
            gate = jnp.where(lane8 == n, g_col[n], gate)
        cnt = jnp.zeros((HEADS, LANES), jnp.int32)
        for m in range(nblk):
            tie = jnp.where((g_col[m] == gate) & (m < lane8), 1, 0)
            cnt = cnt + jnp.where(g_col[m] > gate, 1, tie)
        cnt = jnp.where(lane8 < nblk, cnt, nblk).astype(F32)
        lanef = lane8.astype(F32)
        idx_tile = jnp.zeros((HEADS, LANES), F32)
        for r in range(MOBA_TOPK):
            idx_tile = jnp.where(lane8 == r, jnp.sum(jnp.where(cnt == r, lanef, 0.0), axis=1, keepdims=True), idx_tile)
        idx_ref[0] = idx_tile.astype(jnp.int32)

        sc = []
        for n in range(nblk):
            rank_n = jnp.sum(jnp.where(lane8 == n, cnt, 0.0), axis=1, keepdims=True)
            sc.append(jnp.where(rank_n < MOBA_TOPK, s_blk[n] * MOBA_SCALE, NEG))
        mx = sc[0]
        for n in range(1, nblk):
            mx = jnp.maximum(mx, sc[n])
        mx = jnp.maximum(jnp.max(mx, axis=1, keepdims=True), s_self)
        p_blk = [jnp.exp(s - mx) for s in sc]
        p_self = jnp.exp(s_self - mx)
        tot = p_blk[0]
        for n in range(1, nblk):
            tot = tot + p_blk[n]
        l = jnp.sum(tot, axis=1, keepdims=True) + p_self
        inv = 1.0 / l
        for n in range(nblk):
            p_ref[0, n] = p_blk[n] * inv
        p_self = p_self * inv
        for h in range(HEADS):
            rows = slice(h * HD, (h + 1) * HD)
            ps_ref[rows, :] = ps_ref[rows, :] + jnp.where(done, jnp.broadcast_to(p_self[h:h + 1, :], (HD, LANES)), 0.0)

    finalize()

    @pl.when(t == n_steps - 1)
    def _():
        for cp in copies(0, 1 - half):
            cp.wait()


def _ffn_scores(x, g, wg, wu, wd, page_table, qt, ktn, pool_kt, *, seq0, n_local, pre=None, fnorm=None, name):
    t = x.shape[0]
    n_seq, n_pages = page_table.shape
    n_steps = 2 * n_local
    assert t % n_steps == 0 and n_pages % 4 == 0
    tm = t // n_steps
    nblk = n_pages // 2
    row = lambda i, pt: (i, 0)
    full = lambda i, pt: (0, 0)
    args, specs = _ffn_operands(x, g, wg, wu, wd, tm, pre, fnorm, row)
    args += [qt, ktn, pool_kt]
    specs += [pl.BlockSpec(qt.shape, full), pl.BlockSpec(ktn.shape, full), pl.BlockSpec(memory_space=pl.ANY)]
    grid_spec = pltpu.PrefetchScalarGridSpec(
        num_scalar_prefetch=1, grid=(n_steps,), in_specs=specs,
        out_specs=[pl.BlockSpec((tm, D_MODEL), row),
                   pl.BlockSpec((1, nblk, HEADS, MOBA_BLOCK), lambda i, pt: (i // 2, 0, 0, 0)),
                   pl.BlockSpec((1, HEADS, LANES), lambda i, pt: (i // 2, 0, 0)),
                   pl.BlockSpec((WIDTH, n_seq), full)],
        scratch_shapes=[pltpu.VMEM((2, n_pages // 2, HEADS, HD, PAGE), F32), pltpu.SemaphoreType.DMA((2,)),
                        pltpu.VMEM((WIDTH, LANES), F32), pltpu.VMEM((nblk, HEADS, MOBA_BLOCK), F32)])
    kern = functools.partial(_ffn_scores_kernel, pre_proj=pre is not None, final_norm=fnorm is not None,
                             n_pages=n_pages, seq0=seq0, n_steps=n_steps)
    return pl.pallas_call(
        kern, grid_spec=grid_spec,
        out_shape=[jax.ShapeDtypeStruct((t, D_MODEL), F32),
                   jax.ShapeDtypeStruct((n_local, nblk, HEADS, MOBA_BLOCK), F32),
                   jax.ShapeDtypeStruct((n_local, HEADS, LANES), jnp.int32),
                   jax.ShapeDtypeStruct((WIDTH, n_seq), F32)],
        compiler_params=_cparams(1), name=name)(page_table, *args)


def _value_copies(pt_ref, idx_ref, seq, pool_ref, buf_ref, sem):
    cps = []
    for h in range(HEADS):
        for r in range(MOBA_TOPK):
            blk = idx_ref[seq, h * MOBA_TOPK + r]
            for half in range(2):
                page = pt_ref[seq, 2 * blk + half]
                cps.append(pltpu.make_async_copy(pool_ref.at[page, h], buf_ref.at[h, r, half], sem))
    return cps


def _moba_sample_value_kernel(pt_ref, idx_ref, pa_ref, pb_ref, psa_ref, psb_ref, vtn_ref, pool_ref, o_ref, vbuf, sem,
                              *, n_seq):
    b = pl.program_id(0)
    slot = b % 2
    first = b < n_seq // 2

    @pl.when(b == 0)
    def _():
        for cp in _value_copies(pt_ref, idx_ref, 0, pool_ref, vbuf.at[0], sem.at[0]):
            cp.start()
        o_ref[...] = jnp.zeros(o_ref.shape, F32)

    @pl.when(b + 1 < n_seq)
    def _():
        for cp in _value_copies(pt_ref, idx_ref, b + 1, pool_ref, vbuf.at[1 - slot], sem.at[1 - slot]):
            cp.start()

    for cp in _value_copies(pt_ref, idx_ref, b, pool_ref, vbuf.at[slot], sem.at[slot]):
        cp.wait()

    mine = lax.broadcasted_iota(jnp.int32, (1, LANES), 1) == b
    for h in range(HEADS):
        acc = jnp.zeros((HD, PAGE), F32)
        for r in range(MOBA_TOPK):
            blk = idx_ref[b, h * MOBA_TOPK + r]
            prow = jnp.where(first, pa_ref[0, pl.ds(blk, 1), pl.ds(h, 1), :], pb_ref[0, pl.ds(blk, 1), pl.ds(h, 1), :])
            prow = prow.reshape(1, MOBA_BLOCK)
            for half in range(2):
                acc = acc + prow[:, half * PAGE:(half + 1) * PAGE] * vbuf[slot, h, r, half]
        col = jnp.sum(acc, axis=1, keepdims=True)
        rows = slice(h * HD, (h + 1) * HD)
        o_ref[rows, :] = o_ref[rows, :] + jnp.where(mine, col, 0.0)

    @pl.when(b == n_seq - 1)
    def _():
        o_ref[...] = o_ref[...] + (psa_ref[...] + psb_ref[...]) * vtn_ref[...]


def _moba_sample_values(page_table, idx, p_a, p_b, ps_a, ps_b, vtn, pool_vt):
    n_seq, n_pages = page_table.shape
    nblk = n_pages // 2
    n_half = n_seq // 2
    full = lambda b, pt, ix: (0, 0)
    pblk = (1, nblk, HEADS, MOBA_BLOCK)
    grid_spec = pltpu.PrefetchScalarGridSpec(
        num_scalar_prefetch=2, grid=(n_seq,),
        in_specs=[pl.BlockSpec(pblk, lambda b, pt, ix: (jnp.minimum(b, n_half - 1), 0, 0, 0)),
                  pl.BlockSpec(pblk, lambda b, pt, ix: (jnp.maximum(b - n_half, 0), 0, 0, 0)),
                  pl.BlockSpec(ps_a.shape, full), pl.BlockSpec(ps_b.shape, full), pl.BlockSpec(vtn.shape, full),
                  pl.BlockSpec(memory_space=pl.ANY)],
        out_specs=pl.BlockSpec((WIDTH, n_seq), full),
        scratch_shapes=[pltpu.VMEM((2, HEADS, MOBA_TOPK, 2, HD, PAGE), F32), pltpu.SemaphoreType.DMA((2,))])
    return pl.pallas_call(
        functools.partial(_moba_sample_value_kernel, n_seq=n_seq), grid_spec=grid_spec,
        out_shape=jax.ShapeDtypeStruct((WIDTH, n_seq), F32),
        compiler_params=_cparams(1), name="moba_sample_values")(page_table, idx, p_a, p_b, ps_a, ps_b, vtn, pool_vt)


def _mla_sample_kernel(pt_ref, ql_ref, qr_ref, cn_ref, kn_ref, cpool_ref, rpool_ref, o_ref,
                       cbuf, rbuf, sem, s_ref, *, n_pages, n_seq, chunk_pages):
    b = pl.program_id(0)
    slot = b % 2
    n_chunks = n_pages // chunk_pages
    ct = chunk_pages * PAGE

    def copies(seq, sl):
        cps = []
        for pg in range(n_pages):
            page = pt_ref[seq, pg]
            cps.append(pltpu.make_async_copy(cpool_ref.at[page], cbuf.at[sl, pl.ds(pg * PAGE, PAGE)], sem.at[sl]))
            cps.append(pltpu.make_async_copy(rpool_ref.at[page], rbuf.at[sl, pg], sem.at[sl]))
        return cps

    @pl.when(b == 0)
    def _():
        for cp in copies(0, 0):
            cp.start()

    @pl.when(b + 1 < n_seq)
    def _():
        for cp in copies(b + 1, 1 - slot):
            cp.start()

    ql = ql_ref[0]
    qr = qr_ref[0]
    qlb = ql.astype(BF16)
    qrb = qr.astype(BF16)
    s_self = (jnp.sum(ql * cn_ref[0], axis=1, keepdims=True) + jnp.sum(qr * kn_ref[0], axis=1, keepdims=True))

    for cp in copies(b, slot):
        cp.wait()

    def score_body(c, carry):
        start = pl.multiple_of(c * ct, ct)
        cb = cbuf[slot, pl.ds(start, ct), :].astype(BF16)
        parts = [_dot(qrb, rbuf[slot, c * chunk_pages + j].astype(BF16)) for j in range(chunk_pages)]
        s_ref[c] = _dot_nt(qlb, cb) + jnp.concatenate(parts, axis=1)
        return carry

    lax.fori_loop(0, n_chunks, score_body, 0, unroll=True)

    s3 = s_ref[...]
    mx = jnp.maximum(jnp.max(jnp.max(s3, axis=0), axis=1, keepdims=True), s_self)
    p3 = jnp.exp(s3 - mx)
    p_self = jnp.exp(s_self - mx)
    l = jnp.sum(jnp.sum(p3, axis=0), axis=1, keepdims=True) + p_self
    s_ref[...] = p3

    def pv_body(c, acc):
        start = pl.multiple_of(c * ct, ct)
        cb = cbuf[slot, pl.ds(start, ct), :].astype(BF16)
        return acc + _dot(s_ref[c].astype(BF16), cb)

    acc = lax.fori_loop(0, n_chunks, pv_body, jnp.zeros((HEADS, MLA_KV_RANK), F32), unroll=True)
    o_ref[0] = (acc + p_self * cn_ref[0]) / l


def _mla_sample(page_table, ql, qr, cn, kn, pool_c, pool_rt, chunk_pages=4):
    n_seq, n_pages = page_table.shape
    blk = lambda b, pt: (b, 0, 0)
    n_chunks = n_pages // chunk_pages
    grid_spec = pltpu.PrefetchScalarGridSpec(
        num_scalar_prefetch=1, grid=(n_seq,),
        in_specs=[pl.BlockSpec((1, HEADS, MLA_KV_RANK), blk), pl.BlockSpec((1, HEADS, MLA_ROPE), blk),
                  pl.BlockSpec((1, 1, MLA_KV_RANK), blk), pl.BlockSpec((1, 1, MLA_ROPE), blk),
                  pl.BlockSpec(memory_space=pl.ANY), pl.BlockSpec(memory_space=pl.ANY)],
        out_specs=pl.BlockSpec((1, HEADS, MLA_KV_RANK), blk),
        scratch_shapes=[pltpu.VMEM((2, n_pages * PAGE, MLA_KV_RANK), F32),
                        pltpu.VMEM((2, n_pages, MLA_ROPE, PAGE), F32),
                        pltpu.SemaphoreType.DMA((2,)),
                        pltpu.VMEM((n_chunks, HEADS, chunk_pages * PAGE), F32)])
    return pl.pallas_call(
        functools.partial(_mla_sample_kernel, n_pages=n_pages, n_seq=n_seq, chunk_pages=chunk_pages),
        grid_spec=grid_spec, out_shape=jax.ShapeDtypeStruct((n_seq, HEADS, MLA_KV_RANK), F32),
        compiler_params=_cparams(1), name="mla_sample")(page_table, ql, qr, cn, kn, pool_c, pool_rt)


def _rope_tables(pos, theta, n_rot, period, offset):
    half = n_rot // 2
    inv = theta ** (-jnp.arange(half, dtype=F32) * 2.0 / n_rot)
    ang = pos.astype(F32)[:, None] * inv[None, :]
    cos, sin = jnp.cos(ang), jnp.sin(ang)
    n = pos.shape[0]
    one, zero = jnp.ones((n, 1), F32), jnp.zeros((n, 1), F32)
    unit_c = jnp.concatenate([jnp.broadcast_to(one, (n, offset)), cos, cos,
                              jnp.broadcast_to(one, (n, period - offset - n_rot))], axis=1)
    unit_a = jnp.concatenate([jnp.broadcast_to(zero, (n, offset)), -sin, jnp.broadcast_to(zero, (n, half)),
                              jnp.broadcast_to(zero, (n, period - offset - n_rot))], axis=1)
    unit_b = jnp.concatenate([jnp.broadcast_to(zero, (n, offset)), jnp.broadcast_to(zero, (n, half)), sin,
                              jnp.broadcast_to(zero, (n, period - offset - n_rot))], axis=1)
    rep = LANES // period
    return tuple(jnp.tile(u, (1, rep)) for u in (unit_c, unit_a, unit_b))


def _prep_weights(w_in, mla_w_uq, mla_w_uk, mla_w_uv):
    w = w_in
    q, k, v = w[:, 0:512], w[:, 512:1024], w[:, 1024:1536]
    cq, ckv = w[:, 1536:1920], w[:, 1920:2176]
    kr, gates = w[:, 2176:2208], w[:, 2208:]
    z = lambda n: jnp.zeros((D_MODEL, n), w.dtype)
    w_mix = jnp.concatenate([q, k, v, cq, ckv, z(HD), kr, z(LANES - HD - MLA_ROPE)], axis=1).astype(BF16)
    w_gates = gates.astype(BF16)

    per = HD + MLA_ROPE
    uq = mla_w_uq.reshape(MLA_Q_RANK, HEADS, per)
    wuq = jnp.concatenate([uq, jnp.zeros((MLA_Q_RANK, HEADS, LANES - per), uq.dtype)], axis=2)
    wuq = wuq.reshape(MLA_Q_RANK, HEADS * LANES).astype(BF16)

    uk_pad = jnp.concatenate([mla_w_uk, jnp.zeros((MLA_KV_RANK, HEADS, LANES - HD), mla_w_uk.dtype)], axis=2)
    wuk = uk_pad.reshape(MLA_KV_RANK, HEADS * LANES).astype(BF16)
    wuv = mla_w_uv.reshape(MLA_KV_RANK, WIDTH).astype(BF16)

    eye = jnp.eye(HEADS, dtype=F32)
    ukt = jnp.transpose(mla_w_uk, (1, 2, 0))
    ukt = jnp.concatenate([ukt, jnp.zeros((HEADS, LANES - HD, MLA_KV_RANK), F32)], axis=1)
    wqlat = (ukt[:, :, None, :] * eye[:, None, :, None]).reshape(HEADS * LANES, HEADS * MLA_KV_RANK).astype(BF16)
    lane_r = jnp.zeros((LANES, MLA_ROPE), F32).at[HD + jnp.arange(MLA_ROPE), jnp.arange(MLA_ROPE)].set(1.0)
    pr = (lane_r[None, :, None, :] * eye[:, None, :, None]).reshape(HEADS * LANES, HEADS * MLA_ROPE).astype(BF16)
    pk = jnp.concatenate([lane_r, jnp.zeros((LANES, LANES - MLA_ROPE), F32)], axis=1).astype(BF16)
    uvt = jnp.transpose(mla_w_uv, (1, 0, 2))
    wuvbd = (uvt[:, :, None, :] * eye[:, None, :, None]).reshape(HEADS * MLA_KV_RANK, WIDTH).astype(BF16)
    return w_mix, w_gates, wuq, wuk, wuv, wqlat, pr, pk, wuvbd


def kernel(x_prompt, x_sample, cache_moba_k, cache_moba_v, cache_mla_ckv, cache_mla_krope, cache_mem_k, cache_mem_v, page_table, mem_prompt, ffn1_norm, ffn1_w_gate, ffn1_w_up, ffn1_w_down, mix_norm, w_in, mla_q_norm, mla_w_uq, mla_kv_norm, mla_w_uk, mla_w_uv, moba_w_o, mla_w_o, w_out, xattn_norm, mem_norm, xattn_w_q, xattn_w_k, xattn_w_v, xattn_w_o, ffn2_norm, ffn2_w_gate, ffn2_w_up, ffn2_w_down, final_norm):
    bsz, seq, _ = x_prompt.shape
    n_seq = x_sample.shape[0]
    n_pages = page_table.shape[1]
    past_len = n_pages * PAGE
    assert x_sample.shape[1] == 1 and cache_moba_k.shape[0] == 1 and n_seq == LANES
    tm = min(512, seq)

    bf = lambda a: a[0].astype(BF16)
    w_mix, w_gates, wuq, wuk, wuv, wqlat, pr, pk, wuvbd = _prep_weights(w_in[0], mla_w_uq[0], mla_w_uk[0], mla_w_uv[0])
    f1 = (ffn1_norm, bf(ffn1_w_gate), bf(ffn1_w_up), bf(ffn1_w_down))
    f2 = (ffn2_norm, bf(ffn2_w_gate), bf(ffn2_w_up), bf(ffn2_w_down))
    fn = final_norm.reshape(1, D_MODEL)
    merge_w = (mix_norm, w_gates, bf(moba_w_o), bf(mla_w_o), bf(w_out), xattn_norm, bf(xattn_w_q))
    xwo = bf(xattn_w_o)

    pos_p = jnp.arange(seq, dtype=jnp.int32)
    pos_s = jnp.full((1,), past_len, jnp.int32)
    tabs_p = _rope_tables(pos_p, MOBA_THETA, MOBA_ROT, HD, 0) + _rope_tables(pos_p, MLA_THETA, MLA_ROPE, LANES, HD)
    tabs_s = tuple(jnp.broadcast_to(t, (n_seq, LANES)) for t in
                   _rope_tables(pos_s, MOBA_THETA, MOBA_ROT, HD, 0) + _rope_tables(pos_s, MLA_THETA, MLA_ROPE, LANES, HD))

    xs = x_sample.reshape(n_seq, D_MODEL)
    xs1 = _ffn(xs, *f1, tm=n_seq, name="ffn1_sample")
    qt, ktn, vtn, ckv_s, krt_s, ql, qr, krow = _mix_sample(
        xs1, mix_norm, w_mix, mla_q_norm, wuq, mla_kv_norm, wqlat, pr, pk, tabs_s)
    pool_kt = jnp.transpose(cache_moba_k[0], (0, 2, 3, 1))
    pool_vt = jnp.transpose(cache_moba_v[0], (0, 2, 3, 1))
    pool_rt = jnp.transpose(cache_mla_krope[0], (0, 2, 1))
    n_half = n_seq // 2

    xp = x_prompt.reshape(bsz * seq, D_MODEL)
    x1, p_a, idx_a, ps_a = _ffn_scores(xp, *f1, page_table, qt, ktn, pool_kt, seq0=0, n_local=n_half,
                                       name="ffn1_prompt_scores")
    (qf, kb, kt, vt, vt4, kmean, qcat, kcat, vtm4, ckv_p, krt) = _mix_prompt(
        x1, mix_norm, w_mix, mla_q_norm, wuq, mla_kv_norm, wuk, wuv, tabs_p, bsz=bsz, seq=seq, tm=tm)
    nq = seq // MOBA_BLOCK
    oa_t = _attn_prompt(qf, kb, vt4, kmean.reshape(bsz, nq, WIDTH), bsz=bsz, seq=seq, moba=True)
    ob_t = _attn_prompt(qcat, kcat, vtm4, None, bsz=bsz, seq=seq, moba=False)
    x2, qx = _merge(x1.reshape(bsz, seq, D_MODEL), oa_t, ob_t, None, *merge_w, tm=tm, name="merge_prompt")
    mk, mv, mkb, mvb = _memkv(mem_prompt, mem_norm, bf(xattn_w_k), bf(xattn_w_v))
    ox = _xattn_prompt(qx, mkb, mvb, tm=tm)
    y_prompt, p_b, idx_b, ps_b = _ffn_scores(
        x2.reshape(bsz * seq, D_MODEL), *f2, page_table, qt, ktn, pool_kt, seq0=n_half, n_local=n_seq - n_half,
        pre=(ox.reshape(bsz * seq, WIDTH), xwo), fnorm=fn, name="ffn2_prompt_scores")
    y_prompt = y_prompt.reshape(bsz, seq, D_MODEL)

    p_moba_k = jnp.transpose(kt.reshape(bsz, HEADS, HD, seq), (0, 3, 1, 2))[None]
    p_moba_v = jnp.transpose(vt.reshape(bsz, HEADS, HD, seq), (0, 3, 1, 2))[None]
    p_mla_ckv = ckv_p.reshape(1, bsz, seq, MLA_KV_RANK)
    p_mla_krope = jnp.transpose(krt, (0, 2, 1))[None]
    n_mem = mem_prompt.shape[1]
    p_mem_k = mk.reshape(1, bsz, n_mem, XHEADS, XHD)
    p_mem_v = mv.reshape(1, bsz, n_mem, XHEADS, XHD)

    idx = jnp.concatenate([idx_a, idx_b], axis=0)
    idx2 = idx[:, :, :MOBA_TOPK].reshape(n_seq, HEADS * MOBA_TOPK)
    oa_s = _moba_sample_values(page_table, idx2, p_a, p_b, ps_a, ps_b, vtn, pool_vt)
    olat = _mla_sample(page_table, ql.reshape(n_seq, HEADS, MLA_KV_RANK), qr.reshape(n_seq, HEADS, MLA_ROPE),
                       ckv_s.reshape(n_seq, 1, MLA_KV_RANK), krow[:, :MLA_ROPE].reshape(n_seq, 1, MLA_ROPE),
                       cache_mla_ckv[0], pool_rt)
    xs2, qxs = _merge(xs1.reshape(1, n_seq, D_MODEL), oa_s.reshape(1, WIDTH, n_seq),
                      olat.reshape(1, n_seq, HEADS * MLA_KV_RANK), wuvbd, *merge_w, tm=n_seq, name="merge_sample")
    n_mem_s = cache_mem_k.shape[2]
    memk = cache_mem_k[0].reshape(n_seq, n_mem_s * XHEADS, XHD)
    memv = cache_mem_v[0].reshape(n_seq, n_mem_s * XHEADS, XHD)
    oxs = _xattn_sample(qxs.reshape(n_seq, XHEADS, XHD), memk, memv)
    y_sample = _ffn(xs2.reshape(n_seq, D_MODEL), *f2, tm=n_seq, pre=(oxs.reshape(n_seq, WIDTH).astype(BF16), xwo),
                    fnorm=fn, name="ffn2_sample").reshape(n_seq, 1, D_MODEL)

    s_moba_k = jnp.transpose(ktn.reshape(HEADS, HD, n_seq), (2, 0, 1)).reshape(1, n_seq, 1, HEADS, HD)
    s_moba_v = jnp.transpose(vtn.reshape(HEADS, HD, n_seq), (2, 0, 1)).reshape(1, n_seq, 1, HEADS, HD)
    s_mla_ckv = ckv_s.reshape(1, n_seq, 1, MLA_KV_RANK)
    s_mla_krope = jnp.transpose(krt_s, (1, 0)).reshape(1, n_seq, 1, MLA_ROPE)

    return (y_prompt, y_sample, p_moba_k, p_moba_v, p_mla_ckv, p_mla_krope, p_mem_k, p_mem_v,
            s_moba_k, s_moba_v, s_mla_ckv, s_mla_krope)
```

```python
import functools
import math

import jax
import jax.numpy as jnp
from jax import lax
from jax.experimental import pallas as pl
from jax.experimental.pallas import tpu as pltpu

F32 = jnp.float32
BF16 = jnp.bfloat16

D_MODEL = 1024
PAGE = 128
HEADS = 8
HD = 64
WIDTH = HEADS * HD
MOBA_BLOCK = 256
MOBA_TOPK = 3
MOBA_ROT = HD // 4
MOBA_THETA = 500000.0
MLA_ROPE = 32
MLA_THETA = 10000.0
MLA_Q_RANK = 384
MLA_KV_RANK = 256
XHEADS = 4
XHD = 128
D_FF = 2816
RMS_EPS = 1e-6
LANES = 128
NEG = -1e30
VMEM_LIMIT = 56 * 1024 * 1024

MOBA_SCALE = 1.0 / math.sqrt(HD)
MLA_SCALE = 1.0 / math.sqrt(HD + MLA_ROPE)
X_SCALE = 1.0 / math.sqrt(XHD)
LOG2E = math.log2(math.e)

C_Q, C_K, C_V, C_CQ, C_CKV, C_KR, C_END = 0, 512, 1024, 1536, 1920, 2176, 2304


def _cparams(n_axes=1, vmem=VMEM_LIMIT):
    return pltpu.CompilerParams(dimension_semantics=("arbitrary",) * n_axes, vmem_limit_bytes=vmem)


def _const_spec(shape):
    nd = len(shape)
    return pl.BlockSpec(shape, lambda *_: (0,) * nd, pipeline_mode=pl.Buffered(1))


def _dot(a, b):
    return jnp.dot(a, b, preferred_element_type=F32)


def _dot_nt(a, b):
    return lax.dot_general(a, b, (((1,), (1,)), ((), ())), preferred_element_type=F32)


def _split3(a):
    hi = a.astype(BF16)
    r1 = a - hi.astype(F32)
    mid = r1.astype(BF16)
    lo = (r1 - mid.astype(F32)).astype(BF16)
    return hi, mid, lo


def _dot_nt_f32(a, b):
    ah, am, al = _split3(a)
    bh, bm, bl = _split3(b)
    return (_dot_nt(ah, bh) + (_dot_nt(ah, bm) + _dot_nt(am, bh))
            + (_dot_nt(ah, bl) + _dot_nt(am, bm) + _dot_nt(al, bh)))


def _zero_once_ready(v):
    u = lax.bitcast_convert_type(v, jnp.uint32)
    u = lax.shift_right_logical(lax.shift_right_logical(u, jnp.uint32(16)), jnp.uint32(16))
    return lax.bitcast_convert_type(u, F32)


def _rms(x, g):
    ms = jnp.mean(x * x, axis=-1, keepdims=True)
    return x * lax.rsqrt(ms + RMS_EPS) * g


def _sigmoid(x):
    return 1.0 / (1.0 + jnp.exp(-x))


def _rope_block(x, c, a, b, half):
    up = pltpu.roll(x, LANES - half, 1)
    dn = pltpu.roll(x, half, 1)
    return x * c + up * a + dn * b


def _rope_wide(x, tabs, half):
    c, a, b = tabs
    blocks = [_rope_block(x[:, j:j + LANES], c, a, b, half) for j in range(0, x.shape[1], LANES)]
    return blocks[0] if len(blocks) == 1 else jnp.concatenate(blocks, axis=1)


V_AUG = 16
HEAD_GROUP = 4
FFN_CHUNKS = 2


def _take_ffn_refs(it, pre_proj, final_norm):
    x_ref = next(it)
    pre = (next(it), next(it)) if pre_proj else None
    g_ref, wg_ref, wu_ref, wd_ref = next(it), next(it), next(it), next(it)
    fn_ref = next(it) if final_norm else None
    return x_ref, pre, g_ref, wg_ref, wu_ref, wd_ref, fn_ref


def _ffn_begin(x_ref, pre, g_ref):
    x = x_ref[...]
    if pre is not None:
        o_ref, wo_ref = pre
        x = x + _dot(o_ref[...], wo_ref[...])
    return x, _rms(x, g_ref[...]).astype(BF16)


def _swiglu_act(gt, up):
    return (gt * _sigmoid(gt) * up).astype(BF16)


def _ffn_chunk(h, c, wg_ref, wu_ref, wd_ref):
    fc = D_FF // FFN_CHUNKS
    act = _swiglu_act(_dot(h, wg_ref[:, c * fc:(c + 1) * fc]), _dot(h, wu_ref[:, c * fc:(c + 1) * fc]))
    return _dot(act, wd_ref[c * fc:(c + 1) * fc, :])


def _ffn_end(x, acc, fn_ref):
    y = x + 0.5 * acc
    return y if fn_ref is None else _rms(y, fn_ref[...])


def _ffn_tile(x_ref, pre, g_ref, wg_ref, wu_ref, wd_ref, fn_ref):
    x, h = _ffn_begin(x_ref, pre, g_ref)
    acc = jnp.zeros(x.shape, F32)
    for c in range(FFN_CHUNKS):
        acc = acc + _ffn_chunk(h, c, wg_ref, wu_ref, wd_ref)
    return _ffn_end(x, acc, fn_ref)


def _ffn_kernel(*refs, pre_proj, final_norm):
    it = iter(refs)
    ffn_refs = _take_ffn_refs(it, pre_proj, final_norm)
    out_ref = next(it)
    out_ref[...] = _ffn_tile(*ffn_refs)


def _ffn_operands(x, g, wg, wu, wd, tm, pre, fnorm, row):
    args, specs = [x], [pl.BlockSpec((tm, D_MODEL), row)]
    if pre is not None:
        o, wo = pre
        args += [o, wo]
        specs += [pl.BlockSpec((tm, o.shape[1]), row), _const_spec(wo.shape)]
    args += [g, wg, wu, wd]
    specs += [_const_spec(g.shape), _const_spec(wg.shape), _const_spec(wu.shape), _const_spec(wd.shape)]
    if fnorm is not None:
        args.append(fnorm)
        specs.append(_const_spec(fnorm.shape))
    return args, specs


def _ffn(x, g, wg, wu, wd, *, tm, pre=None, fnorm=None, name):
    t = x.shape[0]
    args, specs = _ffn_operands(x, g, wg, wu, wd, tm, pre, fnorm, lambda i: (i, 0))
    kern = functools.partial(_ffn_kernel, pre_proj=pre is not None, final_norm=fnorm is not None)
    return pl.pallas_call(
        kern, grid=(t // tm,), in_specs=specs,
        out_specs=pl.BlockSpec((tm, D_MODEL), lambda i: (i, 0)),
        out_shape=jax.ShapeDtypeStruct((t, D_MODEL), F32),
        compiler_params=_cparams(1), name=name)(*args)


def _mix_prompt_kernel(x_ref, g_ref, w_ref, qn_ref, wuq_ref, kvn_ref, wuk_ref, wuv_ref,
                       mc_ref, ma_ref, mb_ref, lc_ref, la_ref, lb_ref,
                       qf_ref, kb_ref, kt_ref, vt_ref, vt4_ref, km_ref,
                       qcat_ref, kcat_ref, vtm4_ref, ckv_ref, krt_ref, *, tm):
    h = _rms(x_ref[...], g_ref[...]).astype(BF16)
    proj = _dot(h, w_ref[...])
    mtab = (mc_ref[...], ma_ref[...], mb_ref[...])
    ltab = (lc_ref[...], la_ref[...], lb_ref[...])
    nblk = tm // MOBA_BLOCK

    q = _rope_wide(proj[:, C_Q:C_K], mtab, MOBA_ROT // 2)
    k = _rope_wide(proj[:, C_K:C_V], mtab, MOBA_ROT // 2)
    v = proj[:, C_V:C_CQ]
    qf_ref[...] = q
    kb_ref[...] = k.astype(BF16)
    kt_ref[0] = k.T
    vt = v.T
    vt_ref[0] = vt
    for j in range(nblk):
        vt4_ref[0, j] = vt[:, j * MOBA_BLOCK:(j + 1) * MOBA_BLOCK].astype(BF16)
        km_ref[0, j:j + 1, :] = jnp.sum(k[j * MOBA_BLOCK:(j + 1) * MOBA_BLOCK], axis=0, keepdims=True) * (1.0 / MOBA_BLOCK)

    cqn = _rms(proj[:, C_CQ:C_CKV], qn_ref[...]).astype(BF16)
    qh = _rope_wide(_dot(cqn, wuq_ref[...]), ltab, MLA_ROPE // 2)
    qcat_ref[...] = (qh * (MLA_SCALE * LOG2E)).astype(BF16)

    ckvn = _rms(proj[:, C_CKV:C_KR], kvn_ref[...])
    ckv_ref[...] = ckvn
    kr = _rope_block(proj[:, C_KR:C_END], *ltab, MLA_ROPE // 2)
    krt_ref[0] = kr.T[HD:HD + MLA_ROPE, :]
    cb = ckvn.astype(BF16)
    kn = _dot(cb, wuk_ref[...])
    kcat_ref[...] = (kn + jnp.concatenate([kr] * HEADS, axis=1)).astype(BF16)
    vtm = _dot(cb, wuv_ref[...]).T
    for j in range(nblk):
        vtm4_ref[0, j] = vtm[:, j * MOBA_BLOCK:(j + 1) * MOBA_BLOCK].astype(BF16)


def _mix_prompt(x, g, w, qn, wuq, kvn, wuk, wuv, tabs, *, bsz, seq, tm):
    t = bsz * seq
    spt = seq // tm
    nblk = tm // MOBA_BLOCK
    nb = seq // MOBA_BLOCK
    row = lambda i: (i, 0)
    tab_spec = pl.BlockSpec((tm, LANES), lambda i: (i % spt, 0))
    in_specs = [pl.BlockSpec((tm, D_MODEL), row), _const_spec(g.shape), _const_spec(w.shape),
                _const_spec(qn.shape), _const_spec(wuq.shape), _const_spec(kvn.shape),
                _const_spec(wuk.shape), _const_spec(wuv.shape)] + [tab_spec] * 6
    tcol = lambda i: (i // spt, 0, i % spt)
    t4 = lambda i: (i // spt, i % spt, 0, 0)
    out_shape = [
        jax.ShapeDtypeStruct((t, WIDTH), F32),
        jax.ShapeDtypeStruct((t, WIDTH), BF16),
        jax.ShapeDtypeStruct((bsz, WIDTH, seq), F32),
        jax.ShapeDtypeStruct((bsz, WIDTH, seq), F32),
        jax.ShapeDtypeStruct((bsz, nb, WIDTH, MOBA_BLOCK), BF16),
        jax.ShapeDtypeStruct((t // tm, nblk, WIDTH), F32),
        jax.ShapeDtypeStruct((t, HEADS * LANES), BF16),
        jax.ShapeDtypeStruct((t, HEADS * LANES), BF16),
        jax.ShapeDtypeStruct((bsz, nb, WIDTH, MOBA_BLOCK), BF16),
        jax.ShapeDtypeStruct((t, MLA_KV_RANK), F32),
        jax.ShapeDtypeStruct((bsz, MLA_ROPE, seq), F32),
    ]
    out_specs = [
        pl.BlockSpec((tm, WIDTH), row), pl.BlockSpec((tm, WIDTH), row),
        pl.BlockSpec((1, WIDTH, tm), tcol), pl.BlockSpec((1, WIDTH, tm), tcol),
        pl.BlockSpec((1, nblk, WIDTH, MOBA_BLOCK), t4),
        pl.BlockSpec((1, nblk, WIDTH), lambda i: (i, 0, 0)),
        pl.BlockSpec((tm, HEADS * LANES), row), pl.BlockSpec((tm, HEADS * LANES), row),
        pl.BlockSpec((1, nblk, WIDTH, MOBA_BLOCK), t4),
        pl.BlockSpec((tm, MLA_KV_RANK), row),
        pl.BlockSpec((1, MLA_ROPE, tm), tcol),
    ]
    return pl.pallas_call(
        functools.partial(_mix_prompt_kernel, tm=tm), grid=(t // tm,), in_specs=in_specs,
        out_specs=out_specs, out_shape=out_shape, compiler_params=_cparams(1), name="mix_prompt",
    )(x, g, w, qn, wuq, kvn, wuk, wuv, *tabs)


def _mix_sample_kernel(x_ref, g_ref, w_ref, qn_ref, wuq_ref, kvn_ref, wqlat_ref, pr_ref, pk_ref,
                       mc_ref, ma_ref, mb_ref, lc_ref, la_ref, lb_ref,
                       qt_ref, kt_ref, vt_ref, ckv_ref, krt_ref, ql_ref, qr_ref, krow_ref):
    h = _rms(x_ref[...], g_ref[...]).astype(BF16)
    proj = _dot(h, w_ref[...])
    mtab = (mc_ref[...], ma_ref[...], mb_ref[...])
    ltab = (lc_ref[...], la_ref[...], lb_ref[...])
    qt_ref[...] = _rope_wide(proj[:, C_Q:C_K], mtab, MOBA_ROT // 2).T
    kt_ref[...] = _rope_wide(proj[:, C_K:C_V], mtab, MOBA_ROT // 2).T
    vt_ref[...] = proj[:, C_V:C_CQ].T

    cqn = _rms(proj[:, C_CQ:C_CKV], qn_ref[...]).astype(BF16)
    qh = (_rope_wide(_dot(cqn, wuq_ref[...]), ltab, MLA_ROPE // 2) * MLA_SCALE).astype(BF16)
    ql_ref[...] = _dot(qh, wqlat_ref[...])
    qr_ref[...] = _dot(qh, pr_ref[...])

    ckv_ref[...] = _rms(proj[:, C_CKV:C_KR], kvn_ref[...])
    kr = _rope_block(proj[:, C_KR:C_END], *ltab, MLA_ROPE // 2)
    krt_ref[...] = kr.T[HD:HD + MLA_ROPE, :]
    krow_ref[...] = _dot(kr.astype(BF16), pk_ref[...])


def _mix_sample(x, g, w, qn, wuq, kvn, wqlat, pr, pk, tabs):
    n = x.shape[0]
    args = (x, g, w, qn, wuq, kvn, wqlat, pr, pk) + tuple(tabs)
    out_shape = [
        jax.ShapeDtypeStruct((WIDTH, n), F32), jax.ShapeDtypeStruct((WIDTH, n), F32),
        jax.ShapeDtypeStruct((WIDTH, n), F32), jax.ShapeDtypeStruct((n, MLA_KV_RANK), F32),
        jax.ShapeDtypeStruct((MLA_ROPE, n), F32), jax.ShapeDtypeStruct((n, HEADS * MLA_KV_RANK), F32),
        jax.ShapeDtypeStruct((n, HEADS * MLA_ROPE), F32), jax.ShapeDtypeStruct((n, LANES), F32),
    ]
    return pl.pallas_call(
        _mix_sample_kernel, grid=(1,), in_specs=[_const_spec(a.shape) for a in args],
        out_specs=[_const_spec(s.shape) for s in out_shape], out_shape=out_shape,
        compiler_params=_cparams(1), name="mix_sample")(*args)


def _attn_prompt_kernel(*refs, moba):
    if moba:
        q_ref, k_ref, vt_ref, km_ref, o_ref, m_ref, acc_ref, qs_ref, bias_ref = refs
    else:
        q_ref, k_ref, vt_ref, o_ref, m_ref, acc_ref = refs
    qi = pl.program_id(1)
    blk = MOBA_BLOCK

    if moba:
        nb = km_ref.shape[1]
        lane = lax.broadcasted_iota(jnp.int32, (1, LANES), 1)
        bidx = lax.broadcasted_iota(jnp.int32, (nb, 1), 0)
        past = bidx < qi
        wins_tie = [jnp.where(m < bidx, 1, 0) for m in range(nb)]
        for h in range(HEADS):
            pair = slice((h // 2) * LANES, (h // 2 + 1) * LANES)
            lo = (h % 2) * HD
            qh = jnp.where((lane >= lo) & (lane < lo + HD), q_ref[:, pair], 0.0)
            gate = jnp.where(past, _dot_nt_f32(km_ref[0, :, pair], qh), -jnp.inf)
            cnt = jnp.zeros(gate.shape, jnp.int32)
            for m in range(nb):
                gm = gate[m:m + 1, :]
                cnt = cnt + jnp.where(gm > gate, 1, jnp.where(gm == gate, wins_tie[m], 0))
            bias_ref[h] = jnp.where(past, jnp.where(cnt < MOBA_TOPK, 0.0, NEG), NEG)
            qs_ref[h] = (qh * (MOBA_SCALE * LOG2E)).astype(BF16)

    def q_of(h):
        return qs_ref[h] if moba else q_ref[:, h * LANES:(h + 1) * LANES]

    def k_of(h, j):
        col = h // 2 if moba else h
        return k_ref[pl.ds(pl.multiple_of(j * blk, blk), blk), col * LANES:(col + 1) * LANES]

    ones_rows = jnp.ones((V_AUG, blk), BF16)

    def v_of(h, j):
        return jnp.concatenate([vt_ref[0, j, h * HD:(h + 1) * HD, :], ones_rows], axis=0)

    heads = range(HEADS)
    rows = [slice(h * HD, (h + 1) * HD) for h in heads]

    kpos = lax.broadcasted_iota(jnp.int32, (blk, blk), 0)
    qpos = lax.broadcasted_iota(jnp.int32, (blk, blk), 1)
    causal = kpos <= qpos

    def softmax_piece(h, j, s, own):
        if own:
            s = jnp.where(causal, s, NEG)
            mn = jnp.max(s, axis=0, keepdims=True)
            a = None
        else:
            s = s + bias_ref[h, pl.ds(j, 1), :] if moba else s
            m = m_ref[h, 0:1, :]
            mn = jnp.maximum(m, jnp.max(s, axis=0, keepdims=True))
            a = jnp.exp2(m - mn)
        m_ref[h, 0:1, :] = mn
        return jnp.exp2(s - mn).astype(BF16), a

    def run_blocks(js, own=False):
        units = [(j, range(g, g + HEAD_GROUP)) for j in js for g in range(0, HEADS, HEAD_GROUP)]
        scores, probs = {}, {}
        for k in range(len(units) + 2):
            if k < len(units):
                j, grp = units[k]
                scores[k] = [_dot_nt(k_of(h, j), q_of(h)) for h in grp]
            if 0 <= k - 1 < len(units):
                j, grp = units[k - 1]
                probs[k - 1] = [softmax_piece(h, j, s, own) for h, s in zip(grp, scores.pop(k - 1))]
            if 0 <= k - 2 < len(units):
                j, grp = units[k - 2]
                for h, (p, a) in zip(grp, probs.pop(k - 2)):
                    pv = _dot(v_of(h, j), p)
                    acc_ref[h] = pv if own else acc_ref[h] * a + pv

    run_blocks([qi], own=True)

    def pair_body(t, carry):
        run_blocks([2 * t, 2 * t + 1])
        return carry

    lax.fori_loop(0, lax.shift_right_logical(qi, 1), pair_body, 0)

    @pl.when(lax.rem(qi, 2) == 1)
    def _():
        run_blocks([qi - 1])

    for h in heads:
        a = acc_ref[h]
        o_ref[0, rows[h], :] = a[0:HD] / a[HD:HD + 1]


def _attn_prompt(q, k, vt4, kmean, *, bsz, seq, moba):
    nq = seq // MOBA_BLOCK
    qw = q.shape[1]
    args = [q, k, vt4]
    specs = [pl.BlockSpec((MOBA_BLOCK, qw), lambda b, i: (b * nq + i, 0)),
             pl.BlockSpec((seq, qw), lambda b, i: (b, 0)),
             pl.BlockSpec((1, nq, WIDTH, MOBA_BLOCK), lambda b, i: (b, 0, 0, 0))]
    scratch = [pltpu.VMEM((HEADS, 8, MOBA_BLOCK), F32), pltpu.VMEM((HEADS, HD + V_AUG, MOBA_BLOCK), F32)]
    if moba:
        args.append(kmean)
        specs.append(pl.BlockSpec((1, nq, WIDTH), lambda b, i: (b, 0, 0)))
        scratch += [pltpu.VMEM((HEADS, MOBA_BLOCK, LANES), BF16), pltpu.VMEM((HEADS, nq, MOBA_BLOCK), F32)]
    return pl.pallas_call(
        functools.partial(_attn_prompt_kernel, moba=moba), grid=(bsz, nq),
        in_specs=specs, out_specs=pl.BlockSpec((1, WIDTH, MOBA_BLOCK), lambda b, i: (b, 0, i)),
        out_shape=jax.ShapeDtypeStruct((bsz, WIDTH, seq), F32), scratch_shapes=scratch,
        compiler_params=_cparams(2), name="attn_moba" if moba else "attn_mla")(*args)


def _merge_kernel(*refs, b_latent):
    it = iter(refs)
    x_ref, oa_ref, ob_ref = next(it), next(it), next(it)
    wuvbd_ref = next(it) if b_latent else None
    g_ref, wg_ref, woa_ref, wob_ref, wout_ref, gx_ref, wq_ref, x2_ref, qx_ref = it

    x = x_ref[0]
    h = _rms(x, g_ref[...]).astype(BF16)
    gates = _dot(h, wg_ref[...])
    oa = _dot(oa_ref[0].T.astype(BF16), woa_ref[...])
    if b_latent:
        ob_in = _dot(ob_ref[0].astype(BF16), wuvbd_ref[...]).astype(BF16)
    else:
        ob_in = ob_ref[0].T.astype(BF16)
    ob = _dot(ob_in, wob_ref[...])
    mix = _sigmoid(gates[:, :D_MODEL]) * oa + _sigmoid(gates[:, D_MODEL:]) * ob
    x2 = x + _dot(mix.astype(BF16), wout_ref[...])
    x2_ref[0] = x2
    hx = _rms(x2, gx_ref[...]).astype(BF16)
    qx_ref[0] = (_dot(hx, wq_ref[...]) * X_SCALE).astype(BF16)


def _merge(x3, oa_t, ob, wuvbd, g, wg, woa, wob, wout, gx, wq, *, tm, name):
    bsz, seq, _ = x3.shape
    b_latent = wuvbd is not None
    tok = lambda b, i: (b, i, 0)
    col = lambda b, i: (b, 0, i)
    args = [x3, oa_t, ob]
    specs = [pl.BlockSpec((1, tm, D_MODEL), tok), pl.BlockSpec((1, WIDTH, tm), col)]
    if b_latent:
        specs.append(pl.BlockSpec((1, tm, ob.shape[2]), tok))
        args.append(wuvbd)
        specs.append(_const_spec(wuvbd.shape))
    else:
        specs.append(pl.BlockSpec((1, WIDTH, tm), col))
    consts = [g, wg, woa, wob, wout, gx, wq]
    args += consts
    specs += [_const_spec(a.shape) for a in consts]
    return pl.pallas_call(
        functools.partial(_merge_kernel, b_latent=b_latent), grid=(bsz, seq // tm), in_specs=specs,
        out_specs=[pl.BlockSpec((1, tm, D_MODEL), tok), pl.BlockSpec((1, tm, WIDTH), tok)],
        out_shape=[jax.ShapeDtypeStruct((bsz, seq, D_MODEL), F32), jax.ShapeDtypeStruct((bsz, seq, WIDTH), BF16)],
        compiler_params=_cparams(2), name=name)(*args)


def _memkv_kernel(m_ref, g_ref, wk_ref, wv_ref, k_ref, v_ref, kb_ref, vb_ref):
    hm = _rms(m_ref[0], g_ref[...]).astype(BF16)
    mk = _dot(hm, wk_ref[...])
    mv = _dot(hm, wv_ref[...])
    k_ref[0] = mk
    v_ref[0] = mv
    kb_ref[0] = mk.astype(BF16)
    vb_ref[0] = mv.astype(BF16)


def _memkv(mem, g, wk, wv):
    bsz, nm, _ = mem.shape
    blk = lambda b: (b, 0, 0)
    w = XHEADS * XHD
    return pl.pallas_call(
        _memkv_kernel, grid=(bsz,),
        in_specs=[pl.BlockSpec((1, nm, D_MODEL), blk), _const_spec(g.shape), _const_spec(wk.shape), _const_spec(wv.shape)],
        out_specs=[pl.BlockSpec((1, nm, w), blk)] * 4,
        out_shape=[jax.ShapeDtypeStruct((bsz, nm, w), F32)] * 2 + [jax.ShapeDtypeStruct((bsz, nm, w), BF16)] * 2,
        compiler_params=_cparams(1), name="mem_kv")(mem, g, wk, wv)


def _xattn_prompt_kernel(q_ref, k_ref, v_ref, o_ref):
    outs = []
    for h in range(XHEADS):
        sl = slice(h * XHD, (h + 1) * XHD)
        s = _dot_nt(q_ref[0, :, sl], k_ref[0, :, sl])
        m = jnp.max(s, axis=-1, keepdims=True)
        p = jnp.exp(s - m)
        l = jnp.sum(p, axis=-1, keepdims=True)
        outs.append(_dot(p.astype(BF16), v_ref[0, :, sl]) / l)
    o_ref[0] = jnp.concatenate(outs, axis=1).astype(BF16)


def _xattn_prompt(qx, mk, mv, *, tm):
    bsz, seq, w = qx.shape
    nm = mk.shape[1]
    tok = lambda b, i: (b, i, 0)
    mem = lambda b, i: (b, 0, 0)
    return pl.pallas_call(
        _xattn_prompt_kernel, grid=(bsz, seq // tm),
        in_specs=[pl.BlockSpec((1, tm, w), tok), pl.BlockSpec((1, nm, w), mem), pl.BlockSpec((1, nm, w), mem)],
        out_specs=pl.BlockSpec((1, tm, w), tok), out_shape=jax.ShapeDtypeStruct((bsz, seq, w), BF16),
        compiler_params=_cparams(2), name="xattn_prompt")(qx, mk, mv)


def _xattn_sample_kernel(q_ref, k_ref, v_ref, o_ref):
    rows = k_ref.shape[1]
    col = lax.broadcasted_iota(jnp.int32, (XHEADS, rows), 1)
    hd = lax.broadcasted_iota(jnp.int32, (XHEADS, rows), 0)
    own = (col % XHEADS) == hd
    for g in range(q_ref.shape[0]):
        s = jnp.where(own, _dot_nt(q_ref[g], k_ref[g].astype(BF16)), NEG)
        m = jnp.max(s, axis=-1, keepdims=True)
        p = jnp.exp(s - m)
        l = jnp.sum(p, axis=-1, keepdims=True)
        o_ref[g] = _dot(p.astype(BF16), v_ref[g].astype(BF16)) / l


def _xattn_sample(qx, memk, memv, group=4):
    n = qx.shape[0]
    rows = memk.shape[1]
    blk = lambda b: (b, 0, 0)
    return pl.pallas_call(
        _xattn_sample_kernel, grid=(n // group,),
        in_specs=[pl.BlockSpec((group, XHEADS, XHD), blk), pl.BlockSpec((group, rows, XHD), blk),
                  pl.BlockSpec((group, rows, XHD), blk)],
        out_specs=pl.BlockSpec((group, XHEADS, XHD), blk), out_shape=jax.ShapeDtypeStruct((n, XHEADS, XHD), F32),
        compiler_params=_cparams(1), name="xattn_sample")(qx, memk, memv)


def _ffn_scores_kernel(pt_ref, *refs, pre_proj, final_norm, n_pages, seq0, n_steps):
    it = iter(refs)
    ffn_refs = _take_ffn_refs(it, pre_proj, final_norm)
    qt_ref, ktn_ref, pool_ref = next(it), next(it), next(it)
    y_ref, p_ref, idx_ref, ps_ref = next(it), next(it), next(it), next(it)
    kbuf, sem, qcol_ref, s_ref = it

    t = pl.program_id(0)
    half = t % 2
    seq = seq0 + t // 2
    hp = n_pages // 2
    nblk = n_pages // 2

    def copies(step, slot):
        sq = seq0 + step // 2
        base = (step % 2) * hp
        return [pltpu.make_async_copy(pool_ref.at[pt_ref[sq, base + pg]], kbuf.at[slot, pg], sem.at[slot])
                for pg in range(hp)]

    @pl.when(t == 0)
    def _():
        for cp in copies(0, 0):
            cp.start()
        ps_ref[...] = jnp.zeros(ps_ref.shape, F32)
        s_ref[...] = jnp.zeros(s_ref.shape, F32)

    lane = lax.broadcasted_iota(jnp.int32, (1, LANES), 1)
    mine = lane == seq
    qcol = jnp.sum(jnp.where(mine, qt_ref[...], 0.0), axis=1, keepdims=True)
    qcol_ref[...] = jnp.broadcast_to(qcol, qcol_ref.shape)

    x_ref, pre, g_ref, wg_ref, wu_ref, wd_ref, fn_ref = ffn_refs
    fc = D_FF // FFN_CHUNKS
    corner = lambda a: a[0:8, 0:LANES]
    x, hn = _ffn_begin(x_ref, pre, g_ref)
    act0 = _swiglu_act(_dot(hn, wg_ref[:, 0:fc]), _dot(hn, wu_ref[:, 0:fc]))

    nxt = jnp.where(t + 1 < n_steps, t + 1, 0)
    for cp in copies(nxt, 1 - half):
        cp.start()

    for cp in copies(t, half):
        cp.wait()

    down0 = _dot(act0, wd_ref[0:fc, :])
    gt1 = _dot(hn, wg_ref[:, fc:2 * fc])
    up1 = _dot(hn, wu_ref[:, fc:2 * fc])
    down1 = _dot(_swiglu_act(gt1, up1), wd_ref[fc:2 * fc, :])
    y_ref[...] = _ffn_end(x, down0 + down1, fn_ref)
    after = [None, None, corner(down0), corner(down0), corner(gt1), corner(gt1), corner(up1), corner(up1)]

    for h in range(HEADS):
        qc = qcol_ref[h * HD:(h + 1) * HD, :]
        if after[h] is not None:
            qc = qc + jnp.tile(_zero_once_ready(after[h]), (HD // 8, 1))
        for pg in range(hp):
            row = half * (hp // 2) + pg // 2
            cols = slice((pg % 2) * PAGE, (pg % 2 + 1) * PAGE)
            s_ref[row, h:h + 1, cols] = jnp.sum(kbuf[half, pg, h] * qc, axis=0, keepdims=True)

    done = lane == jnp.where(half == 1, seq, -1)

    def finalize():
        qk = jnp.sum(jnp.where(mine, qt_ref[...] * ktn_ref[...], 0.0), axis=1, keepdims=True)
        sub1 = lax.broadcasted_iota(jnp.int32, (HEADS, 1), 0)
        s_self = jnp.zeros((HEADS, 1), F32)
        for h in range(HEADS):
            s_self = jnp.where(sub1 == h, jnp.sum(qk[h * HD:(h + 1) * HD], axis=0, keepdims=True), s_self)
        s_self = s_self * MOBA_SCALE

        s_blk = [s_ref[n] for n in range(nblk)]
        g_col = [jnp.sum(s, axis=1, keepdims=True) for s in s_blk]
        lane8 = lax.broadcasted_iota(jnp.int32, (HEADS, LANES), 1)
        gate = jnp.zeros((HEADS, LANES), F32)
        for n in range(nblk):
            gate = jnp.where(lane8 == n, g_col[n], gate)
        cnt = jnp.zeros((HEADS, LANES), jnp.int32)
        for m in range(nblk):
            tie = jnp.where((g_col[m] == gate) & (m < lane8), 1, 0)
            cnt = cnt + jnp.where(g_col[m] > gate, 1, tie)
        cnt = jnp.where(lane8 < nblk, cnt, nblk).astype(F32)
        lanef = lane8.astype(F32)
        idx_tile = jnp.zeros((HEADS, LANES), F32)
        for r in range(MOBA_TOPK):
            idx_tile = jnp.where(lane8 == r, jnp.sum(jnp.where(cnt == r, lanef, 0.0), axis=1, keepdims=True), idx_tile)
        idx_ref[0] = idx_tile.astype(jnp.int32)

        sc = []
        for n in range(nblk):
            rank_n = jnp.sum(jnp.where(lane8 == n, cnt, 0.0), axis=1, keepdims=True)
            sc.append(jnp.where(rank_n < MOBA_TOPK, s_blk[n] * MOBA_SCALE, NEG))
        mx = sc[0]
        for n in range(1, nblk):
            mx = jnp.maximum(mx, sc[n])
        mx = jnp.maximum(jnp.max(mx, axis=1, keepdims=True), s_self)
        p_blk = [jnp.exp(s - mx) for s in sc]
        p_self = jnp.exp(s_self - mx)
        tot = p_blk[0]
        for n in range(1, nblk):
            tot = tot + p_blk[n]
        l = jnp.sum(tot, axis=1, keepdims=True) + p_self
        inv = 1.0 / l
        for n in range(nblk):
            p_ref[0, n] = p_blk[n] * inv
        p_self = p_self * inv
        for h in range(HEADS):
            rows = slice(h * HD, (h + 1) * HD)
            ps_ref[rows, :] = ps_ref[rows, :] + jnp.where(done, jnp.broadcast_to(p_self[h:h + 1, :], (HD, LANES)), 0.0)

    finalize()

    @pl.when(t == n_steps - 1)
    def _():
        for cp in copies(0, 1 - half):
            cp.wait()


def _ffn_scores(x, g, wg, wu, wd, page_table, qt, ktn, pool_kt, *, seq0, n_local, pre=None, fnorm=None, name):
    t = x.shape[0]
    n_seq, n_pages = page_table.shape
    n_steps = 2 * n_local
    assert t % n_steps == 0 and n_pages % 4 == 0
    tm = t // n_steps
    nblk = n_pages // 2
    row = lambda i, pt: (i, 0)
    full = lambda i, pt: (0, 0)
    args, specs = _ffn_operands(x, g, wg, wu, wd, tm, pre, fnorm, row)
    args += [qt, ktn, pool_kt]
    specs += [pl.BlockSpec(qt.shape, full), pl.BlockSpec(ktn.shape, full), pl.BlockSpec(memory_space=pl.ANY)]
    grid_spec = pltpu.PrefetchScalarGridSpec(
        num_scalar_prefetch=1, grid=(n_steps,), in_specs=specs,
        out_specs=[pl.BlockSpec((tm, D_MODEL), row),
                   pl.BlockSpec((1, nblk, HEADS, MOBA_BLOCK), lambda i, pt: (i // 2, 0, 0, 0)),
                   pl.BlockSpec((1, HEADS, LANES), lambda i, pt: (i // 2, 0, 0)),
                   pl.BlockSpec((WIDTH, n_seq), full)],
        scratch_shapes=[pltpu.VMEM((2, n_pages // 2, HEADS, HD, PAGE), F32), pltpu.SemaphoreType.DMA((2,)),
                        pltpu.VMEM((WIDTH, LANES), F32), pltpu.VMEM((nblk, HEADS, MOBA_BLOCK), F32)])
    kern = functools.partial(_ffn_scores_kernel, pre_proj=pre is not None, final_norm=fnorm is not None,
                             n_pages=n_pages, seq0=seq0, n_steps=n_steps)
    return pl.pallas_call(
        kern, grid_spec=grid_spec,
        out_shape=[jax.ShapeDtypeStruct((t, D_MODEL), F32),
                   jax.ShapeDtypeStruct((n_local, nblk, HEADS, MOBA_BLOCK), F32),
                   jax.ShapeDtypeStruct((n_local, HEADS, LANES), jnp.int32),
                   jax.ShapeDtypeStruct((WIDTH, n_seq), F32)],
        compiler_params=_cparams(1), name=name)(page_table, *args)


def _value_copies(pt_ref, idx_ref, seq, pool_ref, buf_ref, sem):
    cps = []
    for h in range(HEADS):
        for r in range(MOBA_TOPK):
            blk = idx_ref[seq, h * MOBA_TOPK + r]
            for half in range(2):
                page = pt_ref[seq, 2 * blk + half]
                cps.append(pltpu.make_async_copy(pool_ref.at[page, h], buf_ref.at[h, r, half], sem))
    return cps


def _sample_mla_values_kernel(pt_ref, idx_ref, ql_ref, qr_ref, cn_ref, kn_ref, pa_ref, pb_ref, psa_ref, psb_ref,
                              vtn_ref, cpool_ref, rpool_ref, vpool_ref, o_ref, oval_ref,
                              cbuf, rbuf, vbuf, sem, vsem, s_ref, *, n_pages, n_seq, chunk_pages):
    b = pl.program_id(0)
    slot = b % 2
    n_chunks = n_pages // chunk_pages
    ct = chunk_pages * PAGE

    def copies(seq, sl):
        cps = []
        for pg in range(n_pages):
            page = pt_ref[seq, pg]
            cps.append(pltpu.make_async_copy(cpool_ref.at[page], cbuf.at[sl, pl.ds(pg * PAGE, PAGE)], sem.at[sl]))
            cps.append(pltpu.make_async_copy(rpool_ref.at[page], rbuf.at[sl, pg], sem.at[sl]))
        return cps

    def value_copies(seq, sl):
        return _value_copies(pt_ref, idx_ref, seq, vpool_ref, vbuf.at[sl], vsem.at[sl])

    @pl.when(b == 0)
    def _():
        for cp in value_copies(0, 0) + copies(0, 0):
            cp.start()
        oval_ref[...] = jnp.zeros(oval_ref.shape, F32)

    @pl.when(b + 1 < n_seq)
    def _():
        for cp in value_copies(b + 1, 1 - slot) + copies(b + 1, 1 - slot):
            cp.start()

    for cp in value_copies(b, slot):
        cp.wait()
    first = b < n_seq // 2
    mine = lax.broadcasted_iota(jnp.int32, (1, LANES), 1) == b
    for h in range(HEADS):
        vacc = jnp.zeros((HD, PAGE), F32)
        for r in range(MOBA_TOPK):
            blk = idx_ref[b, h * MOBA_TOPK + r]
            prow = jnp.where(first, pa_ref[0, pl.ds(blk, 1), pl.ds(h, 1), :], pb_ref[0, pl.ds(blk, 1), pl.ds(h, 1), :])
            prow = prow.reshape(1, MOBA_BLOCK)
            for half in range(2):
                vacc = vacc + prow[:, half * PAGE:(half + 1) * PAGE] * vbuf[slot, h, r, half]
        col = jnp.sum(vacc, axis=1, keepdims=True)
        rows = slice(h * HD, (h + 1) * HD)
        oval_ref[rows, :] = oval_ref[rows, :] + jnp.where(mine, col, 0.0)

    @pl.when(b == n_seq - 1)
    def _():
        oval_ref[...] = oval_ref[...] + (psa_ref[...] + psb_ref[...]) * vtn_ref[...]

    ql = ql_ref[0]
    qr = qr_ref[0]
    qlb = ql.astype(BF16)
    qrb = qr.astype(BF16)
    s_self = (jnp.sum(ql * cn_ref[0], axis=1, keepdims=True) + jnp.sum(qr * kn_ref[0], axis=1, keepdims=True))

    for cp in copies(b, slot):
        cp.wait()

    def score_body(c, carry):
        start = pl.multiple_of(c * ct, ct)
        cb = cbuf[slot, pl.ds(start, ct), :].astype(BF16)
        parts = [_dot(qrb, rbuf[slot, c * chunk_pages + j].astype(BF16)) for j in range(chunk_pages)]
        s_ref[c] = _dot_nt(qlb, cb) + jnp.concatenate(parts, axis=1)
        return carry

    lax.fori_loop(0, n_chunks, score_body, 0, unroll=True)

    s3 = s_ref[...]
    mx = jnp.maximum(jnp.max(jnp.max(s3, axis=0), axis=1, keepdims=True), s_self)
    p3 = jnp.exp(s3 - mx)
    p_self = jnp.exp(s_self - mx)
    l = jnp.sum(jnp.sum(p3, axis=0), axis=1, keepdims=True) + p_self
    s_ref[...] = p3

    def pv_body(c, acc):
        start = pl.multiple_of(c * ct, ct)
        cb = cbuf[slot, pl.ds(start, ct), :].astype(BF16)
        return acc + _dot(s_ref[c].astype(BF16), cb)

    acc = lax.fori_loop(0, n_chunks, pv_body, jnp.zeros((HEADS, MLA_KV_RANK), F32), unroll=True)
    o_ref[0] = (acc + p_self * cn_ref[0]) / l


def _sample_mla_values(page_table, idx, ql, qr, cn, kn, p_a, p_b, ps_a, ps_b, vtn, pool_c, pool_rt, pool_vt,
                       chunk_pages=4):
    n_seq, n_pages = page_table.shape
    nblk = n_pages // 2
    n_half = n_seq // 2
    n_chunks = n_pages // chunk_pages
    blk = lambda b, pt, ix: (b, 0, 0)
    full = lambda b, pt, ix: (0, 0)
    pblk = (1, nblk, HEADS, MOBA_BLOCK)
    hbm = pl.BlockSpec(memory_space=pl.ANY)
    grid_spec = pltpu.PrefetchScalarGridSpec(
        num_scalar_prefetch=2, grid=(n_seq,),
        in_specs=[pl.BlockSpec((1, HEADS, MLA_KV_RANK), blk), pl.BlockSpec((1, HEADS, MLA_ROPE), blk),
                  pl.BlockSpec((1, 1, MLA_KV_RANK), blk), pl.BlockSpec((1, 1, MLA_ROPE), blk),
                  pl.BlockSpec(pblk, lambda b, pt, ix: (jnp.minimum(b, n_half - 1), 0, 0, 0)),
                  pl.BlockSpec(pblk, lambda b, pt, ix: (jnp.maximum(b - n_half, 0), 0, 0, 0)),
                  pl.BlockSpec(ps_a.shape, full), pl.BlockSpec(ps_b.shape, full), pl.BlockSpec(vtn.shape, full),
                  hbm, hbm, hbm],
        out_specs=[pl.BlockSpec((1, HEADS, MLA_KV_RANK), blk), pl.BlockSpec((WIDTH, n_seq), full)],
        scratch_shapes=[pltpu.VMEM((2, n_pages * PAGE, MLA_KV_RANK), F32),
                        pltpu.VMEM((2, n_pages, MLA_ROPE, PAGE), F32),
                        pltpu.VMEM((2, HEADS, MOBA_TOPK, 2, HD, PAGE), F32),
                        pltpu.SemaphoreType.DMA((2,)), pltpu.SemaphoreType.DMA((2,)),
                        pltpu.VMEM((n_chunks, HEADS, chunk_pages * PAGE), F32)])
    return pl.pallas_call(
        functools.partial(_sample_mla_values_kernel, n_pages=n_pages, n_seq=n_seq, chunk_pages=chunk_pages),
        grid_spec=grid_spec,
        out_shape=[jax.ShapeDtypeStruct((n_seq, HEADS, MLA_KV_RANK), F32), jax.ShapeDtypeStruct((WIDTH, n_seq), F32)],
        compiler_params=_cparams(1), name="sample_mla_values",
    )(page_table, idx, ql, qr, cn, kn, p_a, p_b, ps_a, ps_b, vtn, pool_c, pool_rt, pool_vt)


def _rope_tables(pos, theta, n_rot, period, offset):
    half = n_rot // 2
    inv = theta ** (-jnp.arange(half, dtype=F32) * 2.0 / n_rot)
    ang = pos.astype(F32)[:, None] * inv[None, :]
    cos, sin = jnp.cos(ang), jnp.sin(ang)
    n = pos.shape[0]
    one, zero = jnp.ones((n, 1), F32), jnp.zeros((n, 1), F32)
    unit_c = jnp.concatenate([jnp.broadcast_to(one, (n, offset)), cos, cos,
                              jnp.broadcast_to(one, (n, period - offset - n_rot))], axis=1)
    unit_a = jnp.concatenate([jnp.broadcast_to(zero, (n, offset)), -sin, jnp.broadcast_to(zero, (n, half)),
                              jnp.broadcast_to(zero, (n, period - offset - n_rot))], axis=1)
    unit_b = jnp.concatenate([jnp.broadcast_to(zero, (n, offset)), jnp.broadcast_to(zero, (n, half)), sin,
                              jnp.broadcast_to(zero, (n, period - offset - n_rot))], axis=1)
    rep = LANES // period
    return tuple(jnp.tile(u, (1, rep)) for u in (unit_c, unit_a, unit_b))


def _prep_weights(w_in, mla_w_uq, mla_w_uk, mla_w_uv):
    w = w_in
    q, k, v = w[:, 0:512], w[:, 512:1024], w[:, 1024:1536]
    cq, ckv = w[:, 1536:1920], w[:, 1920:2176]
    kr, gates = w[:, 2176:2208], w[:, 2208:]
    z = lambda n: jnp.zeros((D_MODEL, n), w.dtype)
    w_mix = jnp.concatenate([q, k, v, cq, ckv, z(HD), kr, z(LANES - HD - MLA_ROPE)], axis=1).astype(BF16)
    w_gates = gates.astype(BF16)

    per = HD + MLA_ROPE
    uq = mla_w_uq.reshape(MLA_Q_RANK, HEADS, per)
    wuq = jnp.concatenate([uq, jnp.zeros((MLA_Q_RANK, HEADS, LANES - per), uq.dtype)], axis=2)
    wuq = wuq.reshape(MLA_Q_RANK, HEADS * LANES).astype(BF16)

    uk_pad = jnp.concatenate([mla_w_uk, jnp.zeros((MLA_KV_RANK, HEADS, LANES - HD), mla_w_uk.dtype)], axis=2)
    wuk = uk_pad.reshape(MLA_KV_RANK, HEADS * LANES).astype(BF16)
    wuv = mla_w_uv.reshape(MLA_KV_RANK, WIDTH).astype(BF16)

    eye = jnp.eye(HEADS, dtype=F32)
    ukt = jnp.transpose(mla_w_uk, (1, 2, 0))
    ukt = jnp.concatenate([ukt, jnp.zeros((HEADS, LANES - HD, MLA_KV_RANK), F32)], axis=1)
    wqlat = (ukt[:, :, None, :] * eye[:, None, :, None]).reshape(HEADS * LANES, HEADS * MLA_KV_RANK).astype(BF16)
    lane_r = jnp.zeros((LANES, MLA_ROPE), F32).at[HD + jnp.arange(MLA_ROPE), jnp.arange(MLA_ROPE)].set(1.0)
    pr = (lane_r[None, :, None, :] * eye[:, None, :, None]).reshape(HEADS * LANES, HEADS * MLA_ROPE).astype(BF16)
    pk = jnp.concatenate([lane_r, jnp.zeros((LANES, LANES - MLA_ROPE), F32)], axis=1).astype(BF16)
    uvt = jnp.transpose(mla_w_uv, (1, 0, 2))
    wuvbd = (uvt[:, :, None, :] * eye[:, None, :, None]).reshape(HEADS * MLA_KV_RANK, WIDTH).astype(BF16)
    return w_mix, w_gates, wuq, wuk, wuv, wqlat, pr, pk, wuvbd


def kernel(x_prompt, x_sample, cache_moba_k, cache_moba_v, cache_mla_ckv, cache_mla_krope, cache_mem_k, cache_mem_v, page_table, mem_prompt, ffn1_norm, ffn1_w_gate, ffn1_w_up, ffn1_w_down, mix_norm, w_in, mla_q_norm, mla_w_uq, mla_kv_norm, mla_w_uk, mla_w_uv, moba_w_o, mla_w_o, w_out, xattn_norm, mem_norm, xattn_w_q, xattn_w_k, xattn_w_v, xattn_w_o, ffn2_norm, ffn2_w_gate, ffn2_w_up, ffn2_w_down, final_norm):
    bsz, seq, _ = x_prompt.shape
    n_seq = x_sample.shape[0]
    n_pages = page_table.shape[1]
    past_len = n_pages * PAGE
    assert x_sample.shape[1] == 1 and cache_moba_k.shape[0] == 1 and n_seq == LANES
    tm = min(512, seq)

    bf = lambda a: a[0].astype(BF16)
    w_mix, w_gates, wuq, wuk, wuv, wqlat, pr, pk, wuvbd = _prep_weights(w_in[0], mla_w_uq[0], mla_w_uk[0], mla_w_uv[0])
    f1 = (ffn1_norm, bf(ffn1_w_gate), bf(ffn1_w_up), bf(ffn1_w_down))
    f2 = (ffn2_norm, bf(ffn2_w_gate), bf(ffn2_w_up), bf(ffn2_w_down))
    fn = final_norm.reshape(1, D_MODEL)
    merge_w = (mix_norm, w_gates, bf(moba_w_o), bf(mla_w_o), bf(w_out), xattn_norm, bf(xattn_w_q))
    xwo = bf(xattn_w_o)

    pos_p = jnp.arange(seq, dtype=jnp.int32)
    pos_s = jnp.full((1,), past_len, jnp.int32)
    tabs_p = _rope_tables(pos_p, MOBA_THETA, MOBA_ROT, HD, 0) + _rope_tables(pos_p, MLA_THETA, MLA_ROPE, LANES, HD)
    tabs_s = tuple(jnp.broadcast_to(t, (n_seq, LANES)) for t in
                   _rope_tables(pos_s, MOBA_THETA, MOBA_ROT, HD, 0) + _rope_tables(pos_s, MLA_THETA, MLA_ROPE, LANES, HD))

    xs = x_sample.reshape(n_seq, D_MODEL)
    xs1 = _ffn(xs, *f1, tm=n_seq, name="ffn1_sample")
    qt, ktn, vtn, ckv_s, krt_s, ql, qr, krow = _mix_sample(
        xs1, mix_norm, w_mix, mla_q_norm, wuq, mla_kv_norm, wqlat, pr, pk, tabs_s)
    pool_kt = jnp.transpose(cache_moba_k[0], (0, 2, 3, 1))
    pool_vt = jnp.transpose(cache_moba_v[0], (0, 2, 3, 1))
    pool_rt = jnp.transpose(cache_mla_krope[0], (0, 2, 1))
    n_half = n_seq // 2

    xp = x_prompt.reshape(bsz * seq, D_MODEL)
    x1, p_a, idx_a, ps_a = _ffn_scores(xp, *f1, page_table, qt, ktn, pool_kt, seq0=0, n_local=n_half,
                                       name="ffn1_prompt_scores")
    (qf, kb, kt, vt, vt4, kmean, qcat, kcat, vtm4, ckv_p, krt) = _mix_prompt(
        x1, mix_norm, w_mix, mla_q_norm, wuq, mla_kv_norm, wuk, wuv, tabs_p, bsz=bsz, seq=seq, tm=tm)
    nq = seq // MOBA_BLOCK
    oa_t = _attn_prompt(qf, kb, vt4, kmean.reshape(bsz, nq, WIDTH), bsz=bsz, seq=seq, moba=True)
    ob_t = _attn_prompt(qcat, kcat, vtm4, None, bsz=bsz, seq=seq, moba=False)
    x2, qx = _merge(x1.reshape(bsz, seq, D_MODEL), oa_t, ob_t, None, *merge_w, tm=tm, name="merge_prompt")
    mk, mv, mkb, mvb = _memkv(mem_prompt, mem_norm, bf(xattn_w_k), bf(xattn_w_v))
    ox = _xattn_prompt(qx, mkb, mvb, tm=tm)
    y_prompt, p_b, idx_b, ps_b = _ffn_scores(
        x2.reshape(bsz * seq, D_MODEL), *f2, page_table, qt, ktn, pool_kt, seq0=n_half, n_local=n_seq - n_half,
        pre=(ox.reshape(bsz * seq, WIDTH), xwo), fnorm=fn, name="ffn2_prompt_scores")
    y_prompt = y_prompt.reshape(bsz, seq, D_MODEL)

    p_moba_k = jnp.transpose(kt.reshape(bsz, HEADS, HD, seq), (0, 3, 1, 2))[None]
    p_moba_v = jnp.transpose(vt.reshape(bsz, HEADS, HD, seq), (0, 3, 1, 2))[None]
    p_mla_ckv = ckv_p.reshape(1, bsz, seq, MLA_KV_RANK)
    p_mla_krope = jnp.transpose(krt, (0, 2, 1))[None]
    n_mem = mem_prompt.shape[1]
    p_mem_k = mk.reshape(1, bsz, n_mem, XHEADS, XHD)
    p_mem_v = mv.reshape(1, bsz, n_mem, XHEADS, XHD)

    idx = jnp.concatenate([idx_a, idx_b], axis=0)
    idx2 = idx[:, :, :MOBA_TOPK].reshape(n_seq, HEADS * MOBA_TOPK)
    olat, oa_s = _sample_mla_values(
        page_table, idx2, ql.reshape(n_seq, HEADS, MLA_KV_RANK), qr.reshape(n_seq, HEADS, MLA_ROPE),
        ckv_s.reshape(n_seq, 1, MLA_KV_RANK), krow[:, :MLA_ROPE].reshape(n_seq, 1, MLA_ROPE),
        p_a, p_b, ps_a, ps_b, vtn, cache_mla_ckv[0], pool_rt, pool_vt)
    xs2, qxs = _merge(xs1.reshape(1, n_seq, D_MODEL), oa_s.reshape(1, WIDTH, n_seq),
                      olat.reshape(1, n_seq, HEADS * MLA_KV_RANK), wuvbd, *merge_w, tm=n_seq, name="merge_sample")
    n_mem_s = cache_mem_k.shape[2]
    memk = cache_mem_k[0].reshape(n_seq, n_mem_s * XHEADS, XHD)
    memv = cache_mem_v[0].reshape(n_seq, n_mem_s * XHEADS, XHD)
    oxs = _xattn_sample(qxs.reshape(n_seq, XHEADS, XHD), memk, memv)
    y_sample = _ffn(xs2.reshape(n_seq, D_MODEL), *f2, tm=n_seq, pre=(oxs.reshape(n_seq, WIDTH).astype(BF16), xwo),
                    fnorm=fn, name="ffn2_sample").reshape(n_seq, 1, D_MODEL)

    s_moba_k = jnp.transpose(ktn.reshape(HEADS, HD, n_seq), (2, 0, 1)).reshape(1, n_seq, 1, HEADS, HD)
    s_moba_v = jnp.transpose(vtn.reshape(HEADS, HD, n_seq), (2, 0, 1)).reshape(1, n_seq, 1, HEADS, HD)
    s_mla_ckv = ckv_s.reshape(1, n_seq, 1, MLA_KV_RANK)
    s_mla_krope = jnp.transpose(krt_s, (1, 0)).reshape(1, n_seq, 1, MLA_ROPE)

    return (y_prompt, y_sample, p_moba_k, p_moba_v, p_mla_ckv, p_mla_krope, p_mem_k, p_mem_v,
            s_moba_k, s_moba_v, s_mla_ckv, s_mla_krope)
```

```python
import functools
import math

import jax
import jax.numpy as jnp
from jax import lax
from jax.experimental import pallas as pl
from jax.experimental.pallas import tpu as pltpu

F32 = jnp.float32
BF16 = jnp.bfloat16

D_MODEL = 1024
PAGE = 128
HEADS = 8
HD = 64
WIDTH = HEADS * HD
MOBA_BLOCK = 256
MOBA_TOPK = 3
MOBA_ROT = HD // 4
MOBA_THETA = 500000.0
MLA_ROPE = 32
MLA_THETA = 10000.0
MLA_Q_RANK = 384
MLA_KV_RANK = 256
XHEADS = 4
XHD = 128
D_FF = 2816
RMS_EPS = 1e-6
LANES = 128
NEG = -1e30
VMEM_LIMIT = 56 * 1024 * 1024

MOBA_SCALE = 1.0 / math.sqrt(HD)
MLA_SCALE = 1.0 / math.sqrt(HD + MLA_ROPE)
X_SCALE = 1.0 / math.sqrt(XHD)
LOG2E = math.log2(math.e)

C_Q, C_K, C_V, C_CQ, C_CKV, C_KR, C_END = 0, 512, 1024, 1536, 1920, 2176, 2304


def _cparams(n_axes=1, vmem=VMEM_LIMIT):
    return pltpu.CompilerParams(dimension_semantics=("arbitrary",) * n_axes, vmem_limit_bytes=vmem)


def _const_spec(shape):
    nd = len(shape)
    return pl.BlockSpec(shape, lambda *_: (0,) * nd, pipeline_mode=pl.Buffered(1))


def _dot(a, b):
    return jnp.dot(a, b, preferred_element_type=F32)


def _dot_nt(a, b):
    return lax.dot_general(a, b, (((1,), (1,)), ((), ())), preferred_element_type=F32)


def _split3(a):
    hi = a.astype(BF16)
    r1 = a - hi.astype(F32)
    mid = r1.astype(BF16)
    lo = (r1 - mid.astype(F32)).astype(BF16)
    return hi, mid, lo


def _dot_nt_f32(a, b):
    ah, am, al = _split3(a)
    bh, bm, bl = _split3(b)
    return (_dot_nt(ah, bh) + (_dot_nt(ah, bm) + _dot_nt(am, bh))
            + (_dot_nt(ah, bl) + _dot_nt(am, bm) + _dot_nt(al, bh)))


def _zero_once_ready(v):
    u = lax.bitcast_convert_type(v, jnp.uint32)
    u = lax.shift_right_logical(lax.shift_right_logical(u, jnp.uint32(16)), jnp.uint32(16))
    return lax.bitcast_convert_type(u, F32)


def _rms(x, g):
    ms = jnp.mean(x * x, axis=-1, keepdims=True)
    return x * lax.rsqrt(ms + RMS_EPS) * g


def _sigmoid(x):
    return 1.0 / (1.0 + jnp.exp(-x))


def _rope_block(x, c, a, b, half):
    up = pltpu.roll(x, LANES - half, 1)
    dn = pltpu.roll(x, half, 1)
    return x * c + up * a + dn * b


def _rope_wide(x, tabs, half):
    c, a, b = tabs
    blocks = [_rope_block(x[:, j:j + LANES], c, a, b, half) for j in range(0, x.shape[1], LANES)]
    return blocks[0] if len(blocks) == 1 else jnp.concatenate(blocks, axis=1)


V_AUG = 16
HEAD_GROUP = 4
FFN_CHUNKS = 2


def _take_ffn_refs(it, pre_proj, final_norm):
    x_ref = next(it)
    pre = (next(it), next(it)) if pre_proj else None
    g_ref, wg_ref, wu_ref, wd_ref = next(it), next(it), next(it), next(it)
    fn_ref = next(it) if final_norm else None
    return x_ref, pre, g_ref, wg_ref, wu_ref, wd_ref, fn_ref


def _ffn_begin(x_ref, pre, g_ref):
    x = x_ref[...]
    if pre is not None:
        o_ref, wo_ref = pre
        x = x + _dot(o_ref[...], wo_ref[...])
    return x, _rms(x, g_ref[...]).astype(BF16)


def _swiglu_act(gt, up):
    return (gt * _sigmoid(gt) * up).astype(BF16)


def _ffn_chunk(h, c, wg_ref, wu_ref, wd_ref):
    fc = D_FF // FFN_CHUNKS
    act = _swiglu_act(_dot(h, wg_ref[:, c * fc:(c + 1) * fc]), _dot(h, wu_ref[:, c * fc:(c + 1) * fc]))
    return _dot(act, wd_ref[c * fc:(c + 1) * fc, :])


def _ffn_end(x, acc, fn_ref):
    y = x + 0.5 * acc
    return y if fn_ref is None else _rms(y, fn_ref[...])


def _ffn_tile(x_ref, pre, g_ref, wg_ref, wu_ref, wd_ref, fn_ref):
    x, h = _ffn_begin(x_ref, pre, g_ref)
    acc = jnp.zeros(x.shape, F32)
    for c in range(FFN_CHUNKS):
        acc = acc + _ffn_chunk(h, c, wg_ref, wu_ref, wd_ref)
    return _ffn_end(x, acc, fn_ref)


def _ffn_kernel(*refs, pre_proj, final_norm):
    it = iter(refs)
    ffn_refs = _take_ffn_refs(it, pre_proj, final_norm)
    out_ref = next(it)
    out_ref[...] = _ffn_tile(*ffn_refs)


def _ffn_operands(x, g, wg, wu, wd, tm, pre, fnorm, row):
    args, specs = [x], [pl.BlockSpec((tm, D_MODEL), row)]
    if pre is not None:
        o, wo = pre
        args += [o, wo]
        specs += [pl.BlockSpec((tm, o.shape[1]), row), _const_spec(wo.shape)]
    args += [g, wg, wu, wd]
    specs += [_const_spec(g.shape), _const_spec(wg.shape), _const_spec(wu.shape), _const_spec(wd.shape)]
    if fnorm is not None:
        args.append(fnorm)
        specs.append(_const_spec(fnorm.shape))
    return args, specs


def _ffn(x, g, wg, wu, wd, *, tm, pre=None, fnorm=None, name):
    t = x.shape[0]
    args, specs = _ffn_operands(x, g, wg, wu, wd, tm, pre, fnorm, lambda i: (i, 0))
    kern = functools.partial(_ffn_kernel, pre_proj=pre is not None, final_norm=fnorm is not None)
    return pl.pallas_call(
        kern, grid=(t // tm,), in_specs=specs,
        out_specs=pl.BlockSpec((tm, D_MODEL), lambda i: (i, 0)),
        out_shape=jax.ShapeDtypeStruct((t, D_MODEL), F32),
        compiler_params=_cparams(1), name=name)(*args)


def _mix_prompt_kernel(x_ref, g_ref, w_ref, qn_ref, wuq_ref, kvn_ref, wuk_ref, wuv_ref,
                       mc_ref, ma_ref, mb_ref, lc_ref, la_ref, lb_ref,
                       qf_ref, kb_ref, kt_ref, vt_ref, vt4_ref, km_ref,
                       qcat_ref, kcat_ref, vtm4_ref, ckv_ref, krt_ref, *, tm):
    h = _rms(x_ref[...], g_ref[...]).astype(BF16)
    proj = _dot(h, w_ref[...])
    mtab = (mc_ref[...], ma_ref[...], mb_ref[...])
    ltab = (lc_ref[...], la_ref[...], lb_ref[...])
    nblk = tm // MOBA_BLOCK

    q = _rope_wide(proj[:, C_Q:C_K], mtab, MOBA_ROT // 2)
    k = _rope_wide(proj[:, C_K:C_V], mtab, MOBA_ROT // 2)
    v = proj[:, C_V:C_CQ]
    qf_ref[...] = q
    kb_ref[...] = k.astype(BF16)
    kt_ref[0] = k.T
    vt = v.T
    vt_ref[0] = vt
    for j in range(nblk):
        vt4_ref[0, j] = vt[:, j * MOBA_BLOCK:(j + 1) * MOBA_BLOCK].astype(BF16)
        km_ref[0, j:j + 1, :] = jnp.sum(k[j * MOBA_BLOCK:(j + 1) * MOBA_BLOCK], axis=0, keepdims=True) * (1.0 / MOBA_BLOCK)

    cqn = _rms(proj[:, C_CQ:C_CKV], qn_ref[...]).astype(BF16)
    qh = _rope_wide(_dot(cqn, wuq_ref[...]), ltab, MLA_ROPE // 2)
    qcat_ref[...] = (qh * (MLA_SCALE * LOG2E)).astype(BF16)

    ckvn = _rms(proj[:, C_CKV:C_KR], kvn_ref[...])
    ckv_ref[...] = ckvn
    kr = _rope_block(proj[:, C_KR:C_END], *ltab, MLA_ROPE // 2)
    krt_ref[0] = kr.T[HD:HD + MLA_ROPE, :]
    cb = ckvn.astype(BF16)
    kn = _dot(cb, wuk_ref[...])
    kcat_ref[...] = (kn + jnp.concatenate([kr] * HEADS, axis=1)).astype(BF16)
    vtm = _dot(cb, wuv_ref[...]).T
    for j in range(nblk):
        vtm4_ref[0, j] = vtm[:, j * MOBA_BLOCK:(j + 1) * MOBA_BLOCK].astype(BF16)


def _mix_prompt(x, g, w, qn, wuq, kvn, wuk, wuv, tabs, *, bsz, seq, tm):
    t = bsz * seq
    spt = seq // tm
    nblk = tm // MOBA_BLOCK
    nb = seq // MOBA_BLOCK
    row = lambda i: (i, 0)
    tab_spec = pl.BlockSpec((tm, LANES), lambda i: (i % spt, 0))
    in_specs = [pl.BlockSpec((tm, D_MODEL), row), _const_spec(g.shape), _const_spec(w.shape),
                _const_spec(qn.shape), _const_spec(wuq.shape), _const_spec(kvn.shape),
                _const_spec(wuk.shape), _const_spec(wuv.shape)] + [tab_spec] * 6
    tcol = lambda i: (i // spt, 0, i % spt)
    t4 = lambda i: (i // spt, i % spt, 0, 0)
    out_shape = [
        jax.ShapeDtypeStruct((t, WIDTH), F32),
        jax.ShapeDtypeStruct((t, WIDTH), BF16),
        jax.ShapeDtypeStruct((bsz, WIDTH, seq), F32),
        jax.ShapeDtypeStruct((bsz, WIDTH, seq), F32),
        jax.ShapeDtypeStruct((bsz, nb, WIDTH, MOBA_BLOCK), BF16),
        jax.ShapeDtypeStruct((t // tm, nblk, WIDTH), F32),
        jax.ShapeDtypeStruct((t, HEADS * LANES), BF16),
        jax.ShapeDtypeStruct((t, HEADS * LANES), BF16),
        jax.ShapeDtypeStruct((bsz, nb, WIDTH, MOBA_BLOCK), BF16),
        jax.ShapeDtypeStruct((t, MLA_KV_RANK), F32),
        jax.ShapeDtypeStruct((bsz, MLA_ROPE, seq), F32),
    ]
    out_specs = [
        pl.BlockSpec((tm, WIDTH), row), pl.BlockSpec((tm, WIDTH), row),
        pl.BlockSpec((1, WIDTH, tm), tcol), pl.BlockSpec((1, WIDTH, tm), tcol),
        pl.BlockSpec((1, nblk, WIDTH, MOBA_BLOCK), t4),
        pl.BlockSpec((1, nblk, WIDTH), lambda i: (i, 0, 0)),
        pl.BlockSpec((tm, HEADS * LANES), row), pl.BlockSpec((tm, HEADS * LANES), row),
        pl.BlockSpec((1, nblk, WIDTH, MOBA_BLOCK), t4),
        pl.BlockSpec((tm, MLA_KV_RANK), row),
        pl.BlockSpec((1, MLA_ROPE, tm), tcol),
    ]
    return pl.pallas_call(
        functools.partial(_mix_prompt_kernel, tm=tm), grid=(t // tm,), in_specs=in_specs,
        out_specs=out_specs, out_shape=out_shape, compiler_params=_cparams(1), name="mix_prompt",
    )(x, g, w, qn, wuq, kvn, wuk, wuv, *tabs)


def _mix_sample_kernel(x_ref, g_ref, w_ref, qn_ref, wuq_ref, kvn_ref, wqlat_ref, pr_ref, pk_ref,
                       mc_ref, ma_ref, mb_ref, lc_ref, la_ref, lb_ref,
                       qt_ref, kt_ref, vt_ref, ckv_ref, krt_ref, ql_ref, qr_ref, krow_ref):
    h = _rms(x_ref[...], g_ref[...]).astype(BF16)
    proj = _dot(h, w_ref[...])
    mtab = (mc_ref[...], ma_ref[...], mb_ref[...])
    ltab = (lc_ref[...], la_ref[...], lb_ref[...])
    qt_ref[...] = _rope_wide(proj[:, C_Q:C_K], mtab, MOBA_ROT // 2).T
    kt_ref[...] = _rope_wide(proj[:, C_K:C_V], mtab, MOBA_ROT // 2).T
    vt_ref[...] = proj[:, C_V:C_CQ].T

    cqn = _rms(proj[:, C_CQ:C_CKV], qn_ref[...]).astype(BF16)
    qh = (_rope_wide(_dot(cqn, wuq_ref[...]), ltab, MLA_ROPE // 2) * MLA_SCALE).astype(BF16)
    ql_ref[...] = _dot(qh, wqlat_ref[...])
    qr_ref[...] = _dot(qh, pr_ref[...])

    ckv_ref[...] = _rms(proj[:, C_CKV:C_KR], kvn_ref[...])
    kr = _rope_block(proj[:, C_KR:C_END], *ltab, MLA_ROPE // 2)
    krt_ref[...] = kr.T[HD:HD + MLA_ROPE, :]
    krow_ref[...] = _dot(kr.astype(BF16), pk_ref[...])


def _mix_sample(x, g, w, qn, wuq, kvn, wqlat, pr, pk, tabs):
    n = x.shape[0]
    args = (x, g, w, qn, wuq, kvn, wqlat, pr, pk) + tuple(tabs)
    out_shape = [
        jax.ShapeDtypeStruct((WIDTH, n), F32), jax.ShapeDtypeStruct((WIDTH, n), F32),
        jax.ShapeDtypeStruct((WIDTH, n), F32), jax.ShapeDtypeStruct((n, MLA_KV_RANK), F32),
        jax.ShapeDtypeStruct((MLA_ROPE, n), F32), jax.ShapeDtypeStruct((n, HEADS * MLA_KV_RANK), F32),
        jax.ShapeDtypeStruct((n, HEADS * MLA_ROPE), F32), jax.ShapeDtypeStruct((n, LANES), F32),
    ]
    return pl.pallas_call(
        _mix_sample_kernel, grid=(1,), in_specs=[_const_spec(a.shape) for a in args],
        out_specs=[_const_spec(s.shape) for s in out_shape], out_shape=out_shape,
        compiler_params=_cparams(1), name="mix_sample")(*args)


def _attn_prompt_kernel(*refs, moba):
    if moba:
        q_ref, k_ref, vt_ref, km_ref, o_ref, m_ref, acc_ref, qs_ref, bias_ref = refs
    else:
        q_ref, k_ref, vt_ref, o_ref, m_ref, acc_ref = refs
    qi = pl.program_id(1)
    blk = MOBA_BLOCK

    if moba:
        nb = km_ref.shape[1]
        lane = lax.broadcasted_iota(jnp.int32, (1, LANES), 1)
        bidx = lax.broadcasted_iota(jnp.int32, (nb, 1), 0)
        past = bidx < qi
        wins_tie = [jnp.where(m < bidx, 1, 0) for m in range(nb)]
        for h in range(HEADS):
            pair = slice((h // 2) * LANES, (h // 2 + 1) * LANES)
            lo = (h % 2) * HD
            qh = jnp.where((lane >= lo) & (lane < lo + HD), q_ref[:, pair], 0.0)
            gate = jnp.where(past, _dot_nt_f32(km_ref[0, :, pair], qh), -jnp.inf)
            cnt = jnp.zeros(gate.shape, jnp.int32)
            for m in range(nb):
                gm = gate[m:m + 1, :]
                cnt = cnt + jnp.where(gm > gate, 1, jnp.where(gm == gate, wins_tie[m], 0))
            bias_ref[h] = jnp.where(past, jnp.where(cnt < MOBA_TOPK, 0.0, NEG), NEG)
            qs_ref[h] = (qh * (MOBA_SCALE * LOG2E)).astype(BF16)

    def q_of(h):
        return qs_ref[h] if moba else q_ref[:, h * LANES:(h + 1) * LANES]

    def k_of(h, j):
        col = h // 2 if moba else h
        return k_ref[pl.ds(pl.multiple_of(j * blk, blk), blk), col * LANES:(col + 1) * LANES]

    ones_rows = jnp.ones((V_AUG, blk), BF16)

    def v_of(h, j):
        return jnp.concatenate([vt_ref[0, j, h * HD:(h + 1) * HD, :], ones_rows], axis=0)

    heads = range(HEADS)
    rows = [slice(h * HD, (h + 1) * HD) for h in heads]

    kpos = lax.broadcasted_iota(jnp.int32, (blk, blk), 0)
    qpos = lax.broadcasted_iota(jnp.int32, (blk, blk), 1)
    causal = kpos <= qpos

    def softmax_piece(h, j, s, own):
        if own:
            s = jnp.where(causal, s, NEG)
            mn = jnp.max(s, axis=0, keepdims=True)
            a = None
        else:
            s = s + bias_ref[h, pl.ds(j, 1), :] if moba else s
            m = m_ref[h, 0:1, :]
            mn = jnp.maximum(m, jnp.max(s, axis=0, keepdims=True))
            a = jnp.exp2(m - mn)
        m_ref[h, 0:1, :] = mn
        return jnp.exp2(s - mn).astype(BF16), a

    def run_blocks(js, own=False):
        units = [(j, range(g, g + HEAD_GROUP)) for j in js for g in range(0, HEADS, HEAD_GROUP)]
        scores, probs = {}, {}
        for k in range(len(units) + 2):
            if k < len(units):
                j, grp = units[k]
                scores[k] = [_dot_nt(k_of(h, j), q_of(h)) for h in grp]
            if 0 <= k - 1 < len(units):
                j, grp = units[k - 1]
                probs[k - 1] = [softmax_piece(h, j, s, own) for h, s in zip(grp, scores.pop(k - 1))]
            if 0 <= k - 2 < len(units):
                j, grp = units[k - 2]
                for h, (p, a) in zip(grp, probs.pop(k - 2)):
                    pv = _dot(v_of(h, j), p)
                    acc_ref[h] = pv if own else acc_ref[h] * a + pv

    run_blocks([qi], own=True)

    def quad_body(t, carry):
        run_blocks([4 * t, 4 * t + 1, 4 * t + 2, 4 * t + 3])
        return carry

    nquad = lax.shift_right_logical(qi, 2)
    lax.fori_loop(0, nquad, quad_body, 0)

    @pl.when((qi & 2) != 0)
    def _():
        run_blocks([4 * nquad, 4 * nquad + 1])

    @pl.when((qi & 1) != 0)
    def _():
        run_blocks([qi - 1])

    for h in heads:
        a = acc_ref[h]
        o_ref[0, rows[h], :] = a[0:HD] / a[HD:HD + 1]


def _attn_prompt(q, k, vt4, kmean, *, bsz, seq, moba):
    nq = seq // MOBA_BLOCK
    qw = q.shape[1]
    args = [q, k, vt4]
    specs = [pl.BlockSpec((MOBA_BLOCK, qw), lambda b, i: (b * nq + i, 0)),
             pl.BlockSpec((seq, qw), lambda b, i: (b, 0)),
             pl.BlockSpec((1, nq, WIDTH, MOBA_BLOCK), lambda b, i: (b, 0, 0, 0))]
    scratch = [pltpu.VMEM((HEADS, 8, MOBA_BLOCK), F32), pltpu.VMEM((HEADS, HD + V_AUG, MOBA_BLOCK), F32)]
    if moba:
        args.append(kmean)
        specs.append(pl.BlockSpec((1, nq, WIDTH), lambda b, i: (b, 0, 0)))
        scratch += [pltpu.VMEM((HEADS, MOBA_BLOCK, LANES), BF16), pltpu.VMEM((HEADS, nq, MOBA_BLOCK), F32)]
    return pl.pallas_call(
        functools.partial(_attn_prompt_kernel, moba=moba), grid=(bsz, nq),
        in_specs=specs, out_specs=pl.BlockSpec((1, WIDTH, MOBA_BLOCK), lambda b, i: (b, 0, i)),
        out_shape=jax.ShapeDtypeStruct((bsz, WIDTH, seq), F32), scratch_shapes=scratch,
        compiler_params=_cparams(2), name="attn_moba" if moba else "attn_mla")(*args)


def _merge_kernel(*refs, b_latent):
    it = iter(refs)
    x_ref, oa_ref, ob_ref = next(it), next(it), next(it)
    wuvbd_ref = next(it) if b_latent else None
    g_ref, wg_ref, woa_ref, wob_ref, wout_ref, gx_ref, wq_ref, x2_ref, qx_ref = it

    x = x_ref[0]
    h = _rms(x, g_ref[...]).astype(BF16)
    gates = _dot(h, wg_ref[...])
    oa = _dot(oa_ref[0].T.astype(BF16), woa_ref[...])
    if b_latent:
        ob_in = _dot(ob_ref[0].astype(BF16), wuvbd_ref[...]).astype(BF16)
    else:
        ob_in = ob_ref[0].T.astype(BF16)
    ob = _dot(ob_in, wob_ref[...])
    mix = _sigmoid(gates[:, :D_MODEL]) * oa + _sigmoid(gates[:, D_MODEL:]) * ob
    x2 = x + _dot(mix.astype(BF16), wout_ref[...])
    x2_ref[0] = x2
    hx = _rms(x2, gx_ref[...]).astype(BF16)
    qx_ref[0] = (_dot(hx, wq_ref[...]) * X_SCALE).astype(BF16)


def _merge(x3, oa_t, ob, wuvbd, g, wg, woa, wob, wout, gx, wq, *, tm, name):
    bsz, seq, _ = x3.shape
    b_latent = wuvbd is not None
    tok = lambda b, i: (b, i, 0)
    col = lambda b, i: (b, 0, i)
    args = [x3, oa_t, ob]
    specs = [pl.BlockSpec((1, tm, D_MODEL), tok), pl.BlockSpec((1, WIDTH, tm), col)]
    if b_latent:
        specs.append(pl.BlockSpec((1, tm, ob.shape[2]), tok))
        args.append(wuvbd)
        specs.append(_const_spec(wuvbd.shape))
    else:
        specs.append(pl.BlockSpec((1, WIDTH, tm), col))
    consts = [g, wg, woa, wob, wout, gx, wq]
    args += consts
    specs += [_const_spec(a.shape) for a in consts]
    return pl.pallas_call(
        functools.partial(_merge_kernel, b_latent=b_latent), grid=(bsz, seq // tm), in_specs=specs,
        out_specs=[pl.BlockSpec((1, tm, D_MODEL), tok), pl.BlockSpec((1, tm, WIDTH), tok)],
        out_shape=[jax.ShapeDtypeStruct((bsz, seq, D_MODEL), F32), jax.ShapeDtypeStruct((bsz, seq, WIDTH), BF16)],
        compiler_params=_cparams(2), name=name)(*args)


def _memkv_kernel(m_ref, g_ref, wk_ref, wv_ref, k_ref, v_ref, kb_ref, vb_ref):
    hm = _rms(m_ref[0], g_ref[...]).astype(BF16)
    mk = _dot(hm, wk_ref[...])
    mv = _dot(hm, wv_ref[...])
    k_ref[0] = mk
    v_ref[0] = mv
    kb_ref[0] = mk.astype(BF16)
    vb_ref[0] = mv.astype(BF16)


def _memkv(mem, g, wk, wv):
    bsz, nm, _ = mem.shape
    blk = lambda b: (b, 0, 0)
    w = XHEADS * XHD
    return pl.pallas_call(
        _memkv_kernel, grid=(bsz,),
        in_specs=[pl.BlockSpec((1, nm, D_MODEL), blk), _const_spec(g.shape), _const_spec(wk.shape), _const_spec(wv.shape)],
        out_specs=[pl.BlockSpec((1, nm, w), blk)] * 4,
        out_shape=[jax.ShapeDtypeStruct((bsz, nm, w), F32)] * 2 + [jax.ShapeDtypeStruct((bsz, nm, w), BF16)] * 2,
        compiler_params=_cparams(1), name="mem_kv")(mem, g, wk, wv)


def _xattn_prompt_kernel(q_ref, k_ref, v_ref, o_ref):
    outs = []
    for h in range(XHEADS):
        sl = slice(h * XHD, (h + 1) * XHD)
        s = _dot_nt(q_ref[0, :, sl], k_ref[0, :, sl])
        m = jnp.max(s, axis=-1, keepdims=True)
        p = jnp.exp(s - m)
        l = jnp.sum(p, axis=-1, keepdims=True)
        outs.append(_dot(p.astype(BF16), v_ref[0, :, sl]) / l)
    o_ref[0] = jnp.concatenate(outs, axis=1).astype(BF16)


def _xattn_prompt(qx, mk, mv, *, tm):
    bsz, seq, w = qx.shape
    nm = mk.shape[1]
    tok = lambda b, i: (b, i, 0)
    mem = lambda b, i: (b, 0, 0)
    return pl.pallas_call(
        _xattn_prompt_kernel, grid=(bsz, seq // tm),
        in_specs=[pl.BlockSpec((1, tm, w), tok), pl.BlockSpec((1, nm, w), mem), pl.BlockSpec((1, nm, w), mem)],
        out_specs=pl.BlockSpec((1, tm, w), tok), out_shape=jax.ShapeDtypeStruct((bsz, seq, w), BF16),
        compiler_params=_cparams(2), name="xattn_prompt")(qx, mk, mv)


def _xattn_sample_kernel(q_ref, k_ref, v_ref, o_ref):
    rows = k_ref.shape[1]
    col = lax.broadcasted_iota(jnp.int32, (XHEADS, rows), 1)
    hd = lax.broadcasted_iota(jnp.int32, (XHEADS, rows), 0)
    own = (col % XHEADS) == hd
    for g in range(q_ref.shape[0]):
        s = jnp.where(own, _dot_nt(q_ref[g], k_ref[g].astype(BF16)), NEG)
        m = jnp.max(s, axis=-1, keepdims=True)
        p = jnp.exp(s - m)
        l = jnp.sum(p, axis=-1, keepdims=True)
        o_ref[g] = _dot(p.astype(BF16), v_ref[g].astype(BF16)) / l


def _xattn_sample(qx, memk, memv, group=8):
    n = qx.shape[0]
    rows = memk.shape[1]
    blk = lambda b: (b, 0, 0)
    return pl.pallas_call(
        _xattn_sample_kernel, grid=(n // group,),
        in_specs=[pl.BlockSpec((group, XHEADS, XHD), blk), pl.BlockSpec((group, rows, XHD), blk),
                  pl.BlockSpec((group, rows, XHD), blk)],
        out_specs=pl.BlockSpec((group, XHEADS, XHD), blk), out_shape=jax.ShapeDtypeStruct((n, XHEADS, XHD), F32),
        compiler_params=_cparams(1), name="xattn_sample")(qx, memk, memv)


def _ffn_scores_kernel(pt_ref, *refs, pre_proj, final_norm, n_pages, seq0, n_steps):
    it = iter(refs)
    ffn_refs = _take_ffn_refs(it, pre_proj, final_norm)
    qt_ref, ktn_ref, pool_ref = next(it), next(it), next(it)
    y_ref, p_ref, idx_ref, ps_ref = next(it), next(it), next(it), next(it)
    kbuf, sem, qcol_ref, s_ref = it

    t = pl.program_id(0)
    half = t % 2
    seq = seq0 + t // 2
    hp = n_pages // 2
    nblk = n_pages // 2

    def copies(step, slot):
        sq = seq0 + step // 2
        base = (step % 2) * hp
        return [pltpu.make_async_copy(pool_ref.at[pt_ref[sq, base + pg]], kbuf.at[slot, pg], sem.at[slot])
                for pg in range(hp)]

    @pl.when(t == 0)
    def _():
        for cp in copies(0, 0):
            cp.start()
        ps_ref[...] = jnp.zeros(ps_ref.shape, F32)
        s_ref[...] = jnp.zeros(s_ref.shape, F32)

    lane = lax.broadcasted_iota(jnp.int32, (1, LANES), 1)
    mine = lane == seq
    qcol = jnp.sum(jnp.where(mine, qt_ref[...], 0.0), axis=1, keepdims=True)
    qcol_ref[...] = jnp.broadcast_to(qcol, qcol_ref.shape)

    x_ref, pre, g_ref, wg_ref, wu_ref, wd_ref, fn_ref = ffn_refs
    fc = D_FF // FFN_CHUNKS
    corner = lambda a: a[0:8, 0:LANES]
    x, hn = _ffn_begin(x_ref, pre, g_ref)
    act0 = _swiglu_act(_dot(hn, wg_ref[:, 0:fc]), _dot(hn, wu_ref[:, 0:fc]))

    nxt = jnp.where(t + 1 < n_steps, t + 1, 0)
    for cp in copies(nxt, 1 - half):
        cp.start()

    for cp in copies(t, half):
        cp.wait()

    down0 = _dot(act0, wd_ref[0:fc, :])
    gt1 = _dot(hn, wg_ref[:, fc:2 * fc])
    up1 = _dot(hn, wu_ref[:, fc:2 * fc])
    down1 = _dot(_swiglu_act(gt1, up1), wd_ref[fc:2 * fc, :])
    y_ref[...] = _ffn_end(x, down0 + down1, fn_ref)
    after = [None, None, corner(down0), corner(down0), corner(gt1), corner(gt1), corner(up1), corner(up1)]

    for h in range(HEADS):
        qc = qcol_ref[h * HD:(h + 1) * HD, :]
        if after[h] is not None:
            qc = qc + jnp.tile(_zero_once_ready(after[h]), (HD // 8, 1))
        for pg in range(hp):
            row = half * (hp // 2) + pg // 2
            cols = slice((pg % 2) * PAGE, (pg % 2 + 1) * PAGE)
            s_ref[row, h:h + 1, cols] = jnp.sum(kbuf[half, pg, h] * qc, axis=0, keepdims=True)

    done = lane == jnp.where(half == 1, seq, -1)

    def finalize():
        qk = jnp.sum(jnp.where(mine, qt_ref[...] * ktn_ref[...], 0.0), axis=1, keepdims=True)
        sub1 = lax.broadcasted_iota(jnp.int32, (HEADS, 1), 0)
        s_self = jnp.zeros((HEADS, 1), F32)
        for h in range(HEADS):
            s_self = jnp.where(sub1 == h, jnp.sum(qk[h * HD:(h + 1) * HD], axis=0, keepdims=True), s_self)
        s_self = s_self * MOBA_SCALE

        s_blk = [s_ref[n] for n in range(nblk)]
        g_col = [jnp.sum(s, axis=1, keepdims=True) for s in s_blk]
        lane8 = lax.broadcasted_iota(jnp.int32, (HEADS, LANES), 1)
        gate = jnp.zeros((HEADS, LANES), F32)
        for n in range(nblk):
            gate = jnp.where(lane8 == n, g_col[n], gate)
        cnt = jnp.zeros((HEADS, LANES), jnp.int32)
        for m in range(nblk):
            tie = jnp.where((g_col[m] == gate) & (m < lane8), 1, 0)
            cnt = cnt + jnp.where(g_col[m] > gate, 1, tie)
        cnt = jnp.where(lane8 < nblk, cnt, nblk).astype(F32)
        lanef = lane8.astype(F32)
        idx_tile = jnp.zeros((HEADS, LANES), F32)
        for r in range(MOBA_TOPK):
            idx_tile = jnp.where(lane8 == r, jnp.sum(jnp.where(cnt == r, lanef, 0.0), axis=1, keepdims=True), idx_tile)
        idx_ref[0] = idx_tile.astype(jnp.int32)

        sc = []
        for n in range(nblk):
            rank_n = jnp.sum(jnp.where(lane8 == n, cnt, 0.0), axis=1, keepdims=True)
            sc.append(jnp.where(rank_n < MOBA_TOPK, s_blk[n] * MOBA_SCALE, NEG))
        mx = sc[0]
        for n in range(1, nblk):
            mx = jnp.maximum(mx, sc[n])
        mx = jnp.maximum(jnp.max(mx, axis=1, keepdims=True), s_self)
        p_blk = [jnp.exp(s - mx) for s in sc]
        p_self = jnp.exp(s_self - mx)
        tot = p_blk[0]
        for n in range(1, nblk):
            tot = tot + p_blk[n]
        l = jnp.sum(tot, axis=1, keepdims=True) + p_self
        inv = 1.0 / l
        for n in range(nblk):
            p_ref[0, n] = p_blk[n] * inv
        p_self = p_self * inv
        for h in range(HEADS):
            rows = slice(h * HD, (h + 1) * HD)
            ps_ref[rows, :] = ps_ref[rows, :] + jnp.where(done, jnp.broadcast_to(p_self[h:h + 1, :], (HD, LANES)), 0.0)

    finalize()

    @pl.when(t == n_steps - 1)
    def _():
        for cp in copies(0, 1 - half):
            cp.wait()


def _ffn_scores(x, g, wg, wu, wd, page_table, qt, ktn, pool_kt, *, seq0, n_local, pre=None, fnorm=None, name):
    t = x.shape[0]
    n_seq, n_pages = page_table.shape
    n_steps = 2 * n_local
    assert t % n_steps == 0 and n_pages % 4 == 0
    tm = t // n_steps
    nblk = n_pages // 2
    row = lambda i, pt: (i, 0)
    full = lambda i, pt: (0, 0)
    args, specs = _ffn_operands(x, g, wg, wu, wd, tm, pre, fnorm, row)
    args += [qt, ktn, pool_kt]
    specs += [pl.BlockSpec(qt.shape, full), pl.BlockSpec(ktn.shape, full), pl.BlockSpec(memory_space=pl.ANY)]
    grid_spec = pltpu.PrefetchScalarGridSpec(
        num_scalar_prefetch=1, grid=(n_steps,), in_specs=specs,
        out_specs=[pl.BlockSpec((tm, D_MODEL), row),
                   pl.BlockSpec((1, nblk, HEADS, MOBA_BLOCK), lambda i, pt: (i // 2, 0, 0, 0)),
                   pl.BlockSpec((1, HEADS, LANES), lambda i, pt: (i // 2, 0, 0)),
                   pl.BlockSpec((WIDTH, n_seq), full)],
        scratch_shapes=[pltpu.VMEM((2, n_pages // 2, HEADS, HD, PAGE), F32), pltpu.SemaphoreType.DMA((2,)),
                        pltpu.VMEM((WIDTH, LANES), F32), pltpu.VMEM((nblk, HEADS, MOBA_BLOCK), F32)])
    kern = functools.partial(_ffn_scores_kernel, pre_proj=pre is not None, final_norm=fnorm is not None,
                             n_pages=n_pages, seq0=seq0, n_steps=n_steps)
    return pl.pallas_call(
        kern, grid_spec=grid_spec,
        out_shape=[jax.ShapeDtypeStruct((t, D_MODEL), F32),
                   jax.ShapeDtypeStruct((n_local, nblk, HEADS, MOBA_BLOCK), F32),
                   jax.ShapeDtypeStruct((n_local, HEADS, LANES), jnp.int32),
                   jax.ShapeDtypeStruct((WIDTH, n_seq), F32)],
        compiler_params=_cparams(1), name=name)(page_table, *args)


def _value_copies(pt_ref, idx_ref, seq, pool_ref, buf_ref, sem):
    cps = []
    for h in range(HEADS):
        for r in range(MOBA_TOPK):
            blk = idx_ref[seq, h * MOBA_TOPK + r]
            for half in range(2):
                page = pt_ref[seq, 2 * blk + half]
                cps.append(pltpu.make_async_copy(pool_ref.at[page, h], buf_ref.at[h, r, half], sem))
    return cps


def _sample_mla_values_kernel(pt_ref, idx_ref, ql_ref, qr_ref, cn_ref, kn_ref, pa_ref, pb_ref, psa_ref, psb_ref,
                              vtn_ref, cpool_ref, rpool_ref, vpool_ref, o_ref, oval_ref,
                              cbuf, rbuf, vbuf, sem, vsem, s_ref, *, n_pages, n_seq, chunk_pages):
    b = pl.program_id(0)
    slot = b % 2
    n_chunks = n_pages // chunk_pages
    ct = chunk_pages * PAGE

    def copies(seq, sl):
        cps = []
        for pg in range(n_pages):
            page = pt_ref[seq, pg]
            cps.append(pltpu.make_async_copy(cpool_ref.at[page], cbuf.at[sl, pl.ds(pg * PAGE, PAGE)], sem.at[sl]))
            cps.append(pltpu.make_async_copy(rpool_ref.at[page], rbuf.at[sl, pg], sem.at[sl]))
        return cps

    def value_copies(seq, sl):
        return _value_copies(pt_ref, idx_ref, seq, vpool_ref, vbuf.at[sl], vsem.at[sl])

    @pl.when(b == 0)
    def _():
        for cp in value_copies(0, 0) + copies(0, 0):
            cp.start()
        oval_ref[...] = jnp.zeros(oval_ref.shape, F32)

    @pl.when(b + 1 < n_seq)
    def _():
        for cp in value_copies(b + 1, 1 - slot) + copies(b + 1, 1 - slot):
            cp.start()

    for cp in value_copies(b, slot):
        cp.wait()
    first = b < n_seq // 2
    mine = lax.broadcasted_iota(jnp.int32, (1, LANES), 1) == b
    for h in range(HEADS):
        vacc = jnp.zeros((HD, PAGE), F32)
        for r in range(MOBA_TOPK):
            blk = idx_ref[b, h * MOBA_TOPK + r]
            prow = jnp.where(first, pa_ref[0, pl.ds(blk, 1), pl.ds(h, 1), :], pb_ref[0, pl.ds(blk, 1), pl.ds(h, 1), :])
            prow = prow.reshape(1, MOBA_BLOCK)
            for half in range(2):
                vacc = vacc + prow[:, half * PAGE:(half + 1) * PAGE] * vbuf[slot, h, r, half]
        col = jnp.sum(vacc, axis=1, keepdims=True)
        rows = slice(h * HD, (h + 1) * HD)
        oval_ref[rows, :] = oval_ref[rows, :] + jnp.where(mine, col, 0.0)

    @pl.when(b == n_seq - 1)
    def _():
        oval_ref[...] = oval_ref[...] + (psa_ref[...] + psb_ref[...]) * vtn_ref[...]

    ql = ql_ref[0]
    qr = qr_ref[0]
    qlb = ql.astype(BF16)
    qrb = qr.astype(BF16)
    s_self = (jnp.sum(ql * cn_ref[0], axis=1, keepdims=True) + jnp.sum(qr * kn_ref[0], axis=1, keepdims=True))

    for cp in copies(b, slot):
        cp.wait()

    def score_body(c, carry):
        start = pl.multiple_of(c * ct, ct)
        cb = cbuf[slot, pl.ds(start, ct), :].astype(BF16)
        parts = [_dot(qrb, rbuf[slot, c * chunk_pages + j].astype(BF16)) for j in range(chunk_pages)]
        s_ref[c] = _dot_nt(qlb, cb) + jnp.concatenate(parts, axis=1)
        return carry

    lax.fori_loop(0, n_chunks, score_body, 0, unroll=True)

    s3 = s_ref[...]
    mx = jnp.maximum(jnp.max(jnp.max(s3, axis=0), axis=1, keepdims=True), s_self)
    p3 = jnp.exp(s3 - mx)
    p_self = jnp.exp(s_self - mx)
    l = jnp.sum(jnp.sum(p3, axis=0), axis=1, keepdims=True) + p_self
    s_ref[...] = p3

    def pv_body(c, acc):
        start = pl.multiple_of(c * ct, ct)
        cb = cbuf[slot, pl.ds(start, ct), :].astype(BF16)
        return acc + _dot(s_ref[c].astype(BF16), cb)

    acc = lax.fori_loop(0, n_chunks, pv_body, jnp.zeros((HEADS, MLA_KV_RANK), F32), unroll=True)
    o_ref[0] = (acc + p_self * cn_ref[0]) / l


def _sample_mla_values(page_table, idx, ql, qr, cn, kn, p_a, p_b, ps_a, ps_b, vtn, pool_c, pool_rt, pool_vt,
                       chunk_pages=4):
    n_seq, n_pages = page_table.shape
    nblk = n_pages // 2
    n_half = n_seq // 2
    n_chunks = n_pages // chunk_pages
    blk = lambda b, pt, ix: (b, 0, 0)
    full = lambda b, pt, ix: (0, 0)
    pblk = (1, nblk, HEADS, MOBA_BLOCK)
    hbm = pl.BlockSpec(memory_space=pl.ANY)
    grid_spec = pltpu.PrefetchScalarGridSpec(
        num_scalar_prefetch=2, grid=(n_seq,),
        in_specs=[pl.BlockSpec((1, HEADS, MLA_KV_RANK), blk), pl.BlockSpec((1, HEADS, MLA_ROPE), blk),
                  pl.BlockSpec((1, 1, MLA_KV_RANK), blk), pl.BlockSpec((1, 1, MLA_ROPE), blk),
                  pl.BlockSpec(pblk, lambda b, pt, ix: (jnp.minimum(b, n_half - 1), 0, 0, 0)),
                  pl.BlockSpec(pblk, lambda b, pt, ix: (jnp.maximum(b - n_half, 0), 0, 0, 0)),
                  pl.BlockSpec(ps_a.shape, full), pl.BlockSpec(ps_b.shape, full), pl.BlockSpec(vtn.shape, full),
                  hbm, hbm, hbm],
        out_specs=[pl.BlockSpec((1, HEADS, MLA_KV_RANK), blk), pl.BlockSpec((WIDTH, n_seq), full)],
        scratch_shapes=[pltpu.VMEM((2, n_pages * PAGE, MLA_KV_RANK), F32),
                        pltpu.VMEM((2, n_pages, MLA_ROPE, PAGE), F32),
                        pltpu.VMEM((2, HEADS, MOBA_TOPK, 2, HD, PAGE), F32),
                        pltpu.SemaphoreType.DMA((2,)), pltpu.SemaphoreType.DMA((2,)),
                        pltpu.VMEM((n_chunks, HEADS, chunk_pages * PAGE), F32)])
    return pl.pallas_call(
        functools.partial(_sample_mla_values_kernel, n_pages=n_pages, n_seq=n_seq, chunk_pages=chunk_pages),
        grid_spec=grid_spec,
        out_shape=[jax.ShapeDtypeStruct((n_seq, HEADS, MLA_KV_RANK), F32), jax.ShapeDtypeStruct((WIDTH, n_seq), F32)],
        compiler_params=_cparams(1), name="sample_mla_values",
    )(page_table, idx, ql, qr, cn, kn, p_a, p_b, ps_a, ps_b, vtn, pool_c, pool_rt, pool_vt)


def _rope_tables(pos, theta, n_rot, period, offset):
    half = n_rot // 2
    inv = theta ** (-jnp.arange(half, dtype=F32) * 2.0 / n_rot)
    ang = pos.astype(F32)[:, None] * inv[None, :]
    cos, sin = jnp.cos(ang), jnp.sin(ang)
    n = pos.shape[0]
    one, zero = jnp.ones((n, 1), F32), jnp.zeros((n, 1), F32)
    unit_c = jnp.concatenate([jnp.broadcast_to(one, (n, offset)), cos, cos,
                              jnp.broadcast_to(one, (n, period - offset - n_rot))], axis=1)
    unit_a = jnp.concatenate([jnp.broadcast_to(zero, (n, offset)), -sin, jnp.broadcast_to(zero, (n, half)),
                              jnp.broadcast_to(zero, (n, period - offset - n_rot))], axis=1)
    unit_b = jnp.concatenate([jnp.broadcast_to(zero, (n, offset)), jnp.broadcast_to(zero, (n, half)), sin,
                              jnp.broadcast_to(zero, (n, period - offset - n_rot))], axis=1)
    rep = LANES // period
    return tuple(jnp.tile(u, (1, rep)) for u in (unit_c, unit_a, unit_b))


def _prep_weights(w_in, mla_w_uq, mla_w_uk, mla_w_uv):
    w = w_in
    q, k, v = w[:, 0:512], w[:, 512:1024], w[:, 1024:1536]
    cq, ckv = w[:, 1536:1920], w[:, 1920:2176]
    kr, gates = w[:, 2176:2208], w[:, 2208:]
    z = lambda n: jnp.zeros((D_MODEL, n), w.dtype)
    w_mix = jnp.concatenate([q, k, v, cq, ckv, z(HD), kr, z(LANES - HD - MLA_ROPE)], axis=1).astype(BF16)
    w_gates = gates.astype(BF16)

    per = HD + MLA_ROPE
    uq = mla_w_uq.reshape(MLA_Q_RANK, HEADS, per)
    wuq = jnp.concatenate([uq, jnp.zeros((MLA_Q_RANK, HEADS, LANES - per), uq.dtype)], axis=2)
    wuq = wuq.reshape(MLA_Q_RANK, HEADS * LANES).astype(BF16)

    uk_pad = jnp.concatenate([mla_w_uk, jnp.zeros((MLA_KV_RANK, HEADS, LANES - HD), mla_w_uk.dtype)], axis=2)
    wuk = uk_pad.reshape(MLA_KV_RANK, HEADS * LANES).astype(BF16)
    wuv = mla_w_uv.reshape(MLA_KV_RANK, WIDTH).astype(BF16)

    eye = jnp.eye(HEADS, dtype=F32)
    ukt = jnp.transpose(mla_w_uk, (1, 2, 0))
    ukt = jnp.concatenate([ukt, jnp.zeros((HEADS, LANES - HD, MLA_KV_RANK), F32)], axis=1)
    wqlat = (ukt[:, :, None, :] * eye[:, None, :, None]).reshape(HEADS * LANES, HEADS * MLA_KV_RANK).astype(BF16)
    lane_r = jnp.zeros((LANES, MLA_ROPE), F32).at[HD + jnp.arange(MLA_ROPE), jnp.arange(MLA_ROPE)].set(1.0)
    pr = (lane_r[None, :, None, :] * eye[:, None, :, None]).reshape(HEADS * LANES, HEADS * MLA_ROPE).astype(BF16)
    pk = jnp.concatenate([lane_r, jnp.zeros((LANES, LANES - MLA_ROPE), F32)], axis=1).astype(BF16)
    uvt = jnp.transpose(mla_w_uv, (1, 0, 2))
    wuvbd = (uvt[:, :, None, :] * eye[:, None, :, None]).reshape(HEADS * MLA_KV_RANK, WIDTH).astype(BF16)
    return w_mix, w_gates, wuq, wuk, wuv, wqlat, pr, pk, wuvbd


def kernel(x_prompt, x_sample, cache_moba_k, cache_moba_v, cache_mla_ckv, cache_mla_krope, cache_mem_k, cache_mem_v, page_table, mem_prompt, ffn1_norm, ffn1_w_gate, ffn1_w_up, ffn1_w_down, mix_norm, w_in, mla_q_norm, mla_w_uq, mla_kv_norm, mla_w_uk, mla_w_uv, moba_w_o, mla_w_o, w_out, xattn_norm, mem_norm, xattn_w_q, xattn_w_k, xattn_w_v, xattn_w_o, ffn2_norm, ffn2_w_gate, ffn2_w_up, ffn2_w_down, final_norm):
    bsz, seq, _ = x_prompt.shape
    n_seq = x_sample.shape[0]
    n_pages = page_table.shape[1]
    past_len = n_pages * PAGE
    assert x_sample.shape[1] == 1 and cache_moba_k.shape[0] == 1 and n_seq == LANES
    tm = min(512, seq)

    bf = lambda a: a[0].astype(BF16)
    w_mix, w_gates, wuq, wuk, wuv, wqlat, pr, pk, wuvbd = _prep_weights(w_in[0], mla_w_uq[0], mla_w_uk[0], mla_w_uv[0])
    f1 = (ffn1_norm, bf(ffn1_w_gate), bf(ffn1_w_up), bf(ffn1_w_down))
    f2 = (ffn2_norm, bf(ffn2_w_gate), bf(ffn2_w_up), bf(ffn2_w_down))
    fn = final_norm.reshape(1, D_MODEL)
    merge_w = (mix_norm, w_gates, bf(moba_w_o), bf(mla_w_o), bf(w_out), xattn_norm, bf(xattn_w_q))
    xwo = bf(xattn_w_o)

    pos_p = jnp.arange(seq, dtype=jnp.int32)
    pos_s = jnp.full((1,), past_len, jnp.int32)
    tabs_p = _rope_tables(pos_p, MOBA_THETA, MOBA_ROT, HD, 0) + _rope_tables(pos_p, MLA_THETA, MLA_ROPE, LANES, HD)
    tabs_s = tuple(jnp.broadcast_to(t, (n_seq, LANES)) for t in
                   _rope_tables(pos_s, MOBA_THETA, MOBA_ROT, HD, 0) + _rope_tables(pos_s, MLA_THETA, MLA_ROPE, LANES, HD))

    xs = x_sample.reshape(n_seq, D_MODEL)
    xs1 = _ffn(xs, *f1, tm=n_seq, name="ffn1_sample")
    qt, ktn, vtn, ckv_s, krt_s, ql, qr, krow = _mix_sample(
        xs1, mix_norm, w_mix, mla_q_norm, wuq, mla_kv_norm, wqlat, pr, pk, tabs_s)
    pool_kt = jnp.transpose(cache_moba_k[0], (0, 2, 3, 1))
    pool_vt = jnp.transpose(cache_moba_v[0], (0, 2, 3, 1))
    pool_rt = jnp.transpose(cache_mla_krope[0], (0, 2, 1))
    n_half = n_seq // 2

    xp = x_prompt.reshape(bsz * seq, D_MODEL)
    x1, p_a, idx_a, ps_a = _ffn_scores(xp, *f1, page_table, qt, ktn, pool_kt, seq0=0, n_local=n_half,
                                       name="ffn1_prompt_scores")
    (qf, kb, kt, vt, vt4, kmean, qcat, kcat, vtm4, ckv_p, krt) = _mix_prompt(
        x1, mix_norm, w_mix, mla_q_norm, wuq, mla_kv_norm, wuk, wuv, tabs_p, bsz=bsz, seq=seq, tm=tm)
    nq = seq // MOBA_BLOCK
    oa_t = _attn_prompt(qf, kb, vt4, kmean.reshape(bsz, nq, WIDTH), bsz=bsz, seq=seq, moba=True)
    ob_t = _attn_prompt(qcat, kcat, vtm4, None, bsz=bsz, seq=seq, moba=False)
    x2, qx = _merge(x1.reshape(bsz, seq, D_MODEL), oa_t, ob_t, None, *merge_w, tm=tm, name="merge_prompt")
    mk, mv, mkb, mvb = _memkv(mem_prompt, mem_norm, bf(xattn_w_k), bf(xattn_w_v))
    ox = _xattn_prompt(qx, mkb, mvb, tm=tm)
    y_prompt, p_b, idx_b, ps_b = _ffn_scores(
        x2.reshape(bsz * seq, D_MODEL), *f2, page_table, qt, ktn, pool_kt, seq0=n_half, n_local=n_seq - n_half,
        pre=(ox.reshape(bsz * seq, WIDTH), xwo), fnorm=fn, name="ffn2_prompt_scores")
    y_prompt = y_prompt.reshape(bsz, seq, D_MODEL)

    p_moba_k = jnp.transpose(kt.reshape(bsz, HEADS, HD, seq), (0, 3, 1, 2))[None]
    p_moba_v = jnp.transpose(vt.reshape(bsz, HEADS, HD, seq), (0, 3, 1, 2))[None]
    p_mla_ckv = ckv_p.reshape(1, bsz, seq, MLA_KV_RANK)
    p_mla_krope = jnp.transpose(krt, (0, 2, 1))[None]
    n_mem = mem_prompt.shape[1]
    p_mem_k = mk.reshape(1, bsz, n_mem, XHEADS, XHD)
    p_mem_v = mv.reshape(1, bsz, n_mem, XHEADS, XHD)

    idx = jnp.concatenate([idx_a, idx_b], axis=0)
    idx2 = idx[:, :, :MOBA_TOPK].reshape(n_seq, HEADS * MOBA_TOPK)
    olat, oa_s = _sample_mla_values(
        page_table, idx2, ql.reshape(n_seq, HEADS, MLA_KV_RANK), qr.reshape(n_seq, HEADS, MLA_ROPE),
        ckv_s.reshape(n_seq, 1, MLA_KV_RANK), krow[:, :MLA_ROPE].reshape(n_seq, 1, MLA_ROPE),
        p_a, p_b, ps_a, ps_b, vtn, cache_mla_ckv[0], pool_rt, pool_vt)
    xs2, qxs = _merge(xs1.reshape(1, n_seq, D_MODEL), oa_s.reshape(1, WIDTH, n_seq),
                      olat.reshape(1, n_seq, HEADS * MLA_KV_RANK), wuvbd, *merge_w, tm=n_seq, name="merge_sample")
    n_mem_s = cache_mem_k.shape[2]
    memk = cache_mem_k[0].reshape(n_seq, n_mem_s * XHEADS, XHD)
    memv = cache_mem_v[0].reshape(n_seq, n_mem_s * XHEADS, XHD)
    oxs = _xattn_sample(qxs.reshape(n_seq, XHEADS, XHD), memk, memv)
    y_sample = _ffn(xs2.reshape(n_seq, D_MODEL), *f2, tm=n_seq, pre=(oxs.reshape(n_seq, WIDTH).astype(BF16), xwo),
                    fnorm=fn, name="ffn2_sample").reshape(n_seq, 1, D_MODEL)

    s_moba_k = jnp.transpose(ktn.reshape(HEADS, HD, n_seq), (2, 0, 1)).reshape(1, n_seq, 1, HEADS, HD)
    s_moba_v = jnp.transpose(vtn.reshape(HEADS, HD, n_seq), (2, 0, 1)).reshape(1, n_seq, 1, HEADS, HD)
    s_mla_ckv = ckv_s.reshape(1, n_seq, 1, MLA_KV_RANK)
    s_mla_krope = jnp.transpose(krt_s, (1, 0)).reshape(1, n_seq, 1, MLA_ROPE)

    return (y_prompt, y_sample, p_moba_k, p_moba_v, p_mla_ckv, p_mla_krope, p_mem_k, p_mem_v,
            s_moba_k, s_moba_v, s_mla_ckv, s_mla_krope)
```
